```python
import jax, jax.numpy as jnp
from jax import lax
import numpy as np

D_MODEL = 2048
BATCH = 4
SEQ = 4096
DEPTH = 4

N_MIXERS = 3
EPS = 1e-6
M_HEADS = 8
M_DV = D_MODEL // M_HEADS
M_DQK = M_DV // 2
M_CHUNK = 64
F_BIAS_INIT = 3.0
A_HEADS = 16
A_DHEAD = D_MODEL // A_HEADS
IDX_HEADS = 16
IDX_DIM = 64
TOPK_MAX = 256
Q_BLOCK = 128
CONV_WIDTH = 31
D_FF = 5632
FFN_CONV_WIDTH = 3

kernel_name = "hybrid_mlstm_dsa_conformer_convffn"


def n_layers_of(kind):
    return (DEPTH - kind + N_MIXERS - 1) // N_MIXERS


def rms_norm(x, g):
    xf = x.astype(jnp.float32)
    y = xf * lax.rsqrt(jnp.mean(xf * xf, -1, keepdims=True) + EPS)
    return (y * g.astype(jnp.float32)).astype(x.dtype)


def layer_norm(x, g, b):
    xf = x.astype(jnp.float32)
    mu = jnp.mean(xf, -1, keepdims=True)
    xc = xf - mu
    y = xc * lax.rsqrt(jnp.mean(xc * xc, -1, keepdims=True) + EPS)
    return (y * g.astype(jnp.float32) + b.astype(jnp.float32)).astype(x.dtype)


def causal_depthwise_conv(x, w):
    width, ch = w.shape
    return lax.conv_general_dilated(
        x, w[:, None, :].astype(x.dtype), window_strides=(1,), padding=[(width - 1, 0)],
        dimension_numbers=('NWC', 'WIO', 'NWC'), feature_group_count=ch)


def mlstm_chunkwise(q, k, v, log_i, log_f):
    B, H, T, dk = q.shape
    dv = v.shape[-1]
    L = M_CHUNK
    nc = T // L

    def chunks(a):
        return jnp.moveaxis(a.reshape(B, H, nc, L, *a.shape[3:]), 2, 0)

    causal = jnp.tril(jnp.ones((L, L), dtype=bool))

    def step(carry, xs):
        C, n, m = carry
        qb, kb, vb, ib, fb = xs
        b = jnp.cumsum(fb, axis=-1)
        d = b[..., :, None] - b[..., None, :] + ib[..., None, :]
        d = jnp.where(causal, d, -jnp.inf)
        inter = b + m[..., None]
        m_t = jnp.maximum(inter, jnp.max(d, axis=-1))
        w_intra = jnp.exp(d - m_t[..., None])
        w_inter = jnp.exp(inter - m_t)
        s = jnp.einsum('bhtd,bhsd->bhts', qb, kb) * w_intra
        num = (w_inter[..., None] * jnp.einsum('bhtd,bhde->bhte', qb, C)
               + jnp.einsum('bhts,bhse->bhte', s, vb))
        den = w_inter * jnp.einsum('bhtd,bhd->bht', qb, n) + jnp.sum(s, axis=-1)
        h = num / jnp.maximum(jnp.abs(den), jnp.exp(-m_t))[..., None]
        bl = b[..., -1]
        g = bl[..., None] - b + ib
        m_new = jnp.maximum(bl + m, jnp.max(g, axis=-1))
        a_prev = jnp.exp(bl + m - m_new)
        a_s = jnp.exp(g - m_new[..., None])
        C_new = a_prev[..., None, None] * C + jnp.einsum('bhs,bhsd,bhse->bhde', a_s, kb, vb)
        n_new = a_prev[..., None] * n + jnp.einsum('bhs,bhsd->bhd', a_s, kb)
        return (C_new, n_new, m_new), h

    init = (jnp.zeros((B, H, dk, dv), jnp.float32), jnp.zeros((B, H, dk), jnp.float32),
            jnp.zeros((B, H), jnp.float32))
    _, h = lax.scan(step, init, tuple(map(chunks, (q, k, v, log_i, log_f))))
    return jnp.moveaxis(h, 0, 2).reshape(B, H, T, dv)


def mlstm_mixer(x, w_in, gate_b, head_g, w_out):
    B, T, _ = x.shape
    hk = M_HEADS * M_DQK
    hv = M_HEADS * M_DV
    proj = x @ w_in
    q, k, v, o, gates = jnp.split(proj, [hk, 2 * hk, 2 * hk + hv, 2 * hk + 2 * hv], axis=-1)

    def heads(a, d):
        return a.reshape(B, T, M_HEADS, d).transpose(0, 2, 1, 3).astype(jnp.float32)

    q = heads(q, M_DQK) * (M_DQK ** -0.5)
    k = heads(k, M_DQK)
    v = heads(v, M_DV)
    gates = (gates + gate_b).astype(jnp.float32).transpose(0, 2, 1)
    log_i = gates[:, :M_HEADS]
    log_f = jax.nn.log_sigmoid(gates[:, M_HEADS:])
    h = mlstm_chunkwise(q, k, v, log_i, log_f).transpose(0, 2, 1, 3)
    h = h * lax.rsqrt(jnp.mean(h * h, -1, keepdims=True) + EPS) * head_g.reshape(M_HEADS, M_DV).astype(jnp.float32)
    h = h.reshape(B, T, hv) * jax.nn.sigmoid(o.astype(jnp.float32))
    return h.astype(x.dtype) @ w_out


def dsa_mixer(x, w_in, q_g, k_g, w_out):
    B, T, _ = x.shape
    topk = min(TOPK_MAX, T // 4)
    qd = A_HEADS * A_DHEAD
    qid = IDX_HEADS * IDX_DIM
    cuts = list(np.cumsum([qd, A_DHEAD, A_DHEAD, qid, IDX_DIM]))
    q, k, v, qi, ki, wi = jnp.split(x @ w_in, [int(c) for c in cuts], axis=-1)
    q = rms_norm(q.reshape(B, T, A_HEADS, A_DHEAD), q_g) * (A_DHEAD ** -0.5)
    k = rms_norm(k, k_g)
    qi = qi.reshape(B, T, IDX_HEADS, IDX_DIM)
    wi = wi.astype(jnp.float32) * ((IDX_HEADS * IDX_DIM) ** -0.5)
    nb = T // Q_BLOCK
    key_pos = jnp.arange(T)

    def blocks(a):
        return jnp.moveaxis(a.reshape(B, nb, Q_BLOCK, *a.shape[2:]), 1, 0)

    def attend(xs):
        qb, qib, wib, q0 = xs
        qpos = q0 + jnp.arange(Q_BLOCK)
        idx_logits = jnp.einsum('bqhd,bsd->bqhs', qib, ki)
        score = jnp.einsum('bqhs,bqh->bqs', jax.nn.relu(idx_logits).astype(jnp.float32), wib)
        score = jnp.where(key_pos[None, None, :] <= qpos[None, :, None], score, -jnp.inf)
        _, sel = lax.top_k(score, topk)
        valid = sel <= qpos[None, :, None]
        k_sel = jax.vmap(lambda kk, ii: kk[ii])(k, sel)
        v_sel = jax.vmap(lambda vv, ii: vv[ii])(v, sel)
        logits = jnp.einsum('bqhd,bqkd->bhqk', qb, k_sel).astype(jnp.float32)
        logits = jnp.where(valid[:, None], logits, -jnp.inf)
        p = jax.nn.softmax(logits, axis=-1).astype(v.dtype)
        return jnp.einsum('bhqk,bqkd->bqhd', p, v_sel)

    out = lax.map(attend, (blocks(q), blocks(qi), blocks(wi), jnp.arange(nb) * Q_BLOCK))
    out = jnp.moveaxis(out, 0, 1).reshape(B, T, qd)
    return out @ w_out


def conv_mixer(x, w_in, dw_w, dw_b, ln_g, ln_b, w_out):
    a, g = jnp.split(x @ w_in, 2, axis=-1)
    u = a * jax.nn.sigmoid(g)
    u = causal_depthwise_conv(u, dw_w) + dw_b
    u = jax.nn.silu(layer_norm(u, ln_g, ln_b))
    return u @ w_out


def conv_ffn(x, w_up, dw_w, w_down):
    h = causal_depthwise_conv(x @ w_up, dw_w)
    a, b = jnp.split(h, 2, axis=-1)
    return (jax.nn.silu(a) * b) @ w_down


def setup_inputs(seed: int = 0) -> dict:
    key = jax.random.key(seed)
    ks = jax.random.split(key, 24)
    D = D_MODEL
    na, nb_, nc = n_layers_of(0), n_layers_of(1), n_layers_of(2)
    out_scale = (2 * DEPTH) ** -0.5

    def nrm(k, shape, scale):
        return jax.random.normal(k, shape, jnp.float32) * scale

    pa = 2 * M_HEADS * M_DQK + 2 * M_HEADS * M_DV + 2 * M_HEADS
    pb = A_HEADS * A_DHEAD + 2 * A_DHEAD + IDX_HEADS * IDX_DIM + IDX_DIM + IDX_HEADS
    gate_b = jnp.concatenate([nrm(ks[3], (na, M_HEADS), 0.1),
                              F_BIAS_INIT + nrm(ks[4], (na, M_HEADS), 0.1)], axis=-1)
    return {
        'x': nrm(ks[0], (BATCH, SEQ, D), 1.0),
        'mix_norm_g': 1.0 + nrm(ks[1], (DEPTH, D), 0.02),
        'ffn_norm_g': 1.0 + nrm(ks[2], (DEPTH, D), 0.02),
        'mlstm_w_in': nrm(ks[5], (na, D, pa), D ** -0.5),
        'mlstm_gate_b': gate_b,
        'mlstm_head_g': 1.0 + nrm(ks[6], (na, M_HEADS * M_DV), 0.02),
        'mlstm_w_out': nrm(ks[7], (na, M_HEADS * M_DV, D), (M_HEADS * M_DV) ** -0.5 * out_scale),
        'dsa_w_in': nrm(ks[8], (nb_, D, pb), D ** -0.5),
        'dsa_q_g': 1.0 + nrm(ks[9], (nb_, A_DHEAD), 0.02),
        'dsa_k_g': 1.0 + nrm(ks[10], (nb_, A_DHEAD), 0.02),
        'dsa_w_out': nrm(ks[11], (nb_, A_HEADS * A_DHEAD, D), (A_HEADS * A_DHEAD) ** -0.5 * out_scale),
        'conv_w_in': nrm(ks[12], (nc, D, 2 * D), D ** -0.5),
        'conv_dw_w': nrm(ks[13], (nc, CONV_WIDTH, D), CONV_WIDTH ** -0.5),
        'conv_dw_b': nrm(ks[14], (nc, D), 0.02),
        'conv_ln_g': 1.0 + nrm(ks[15], (nc, D), 0.02),
        'conv_ln_b': nrm(ks[16], (nc, D), 0.02),
        'conv_w_out': nrm(ks[17], (nc, D, D), D ** -0.5 * out_scale),
        'ffn_w_up': nrm(ks[18], (DEPTH, D, 2 * D_FF), D ** -0.5),
        'ffn_dw_w': nrm(ks[19], (DEPTH, FFN_CONV_WIDTH, 2 * D_FF), FFN_CONV_WIDTH ** -0.5),
        'ffn_w_down': nrm(ks[20], (DEPTH, D_FF, D), D_FF ** -0.5 * out_scale),
    }


def reference(x, mix_norm_g, ffn_norm_g, mlstm_w_in, mlstm_gate_b, mlstm_head_g, mlstm_w_out,
              dsa_w_in, dsa_q_g, dsa_k_g, dsa_w_out, conv_w_in, conv_dw_w, conv_dw_b,
              conv_ln_g, conv_ln_b, conv_w_out, ffn_w_up, ffn_dw_w, ffn_w_down):
    for i in range(DEPTH):
        kind = i % N_MIXERS
        j = i // N_MIXERS
        h = rms_norm(x, mix_norm_g[i])
        if kind == 0:
            y = mlstm_mixer(h, mlstm_w_in[j], mlstm_gate_b[j], mlstm_head_g[j], mlstm_w_out[j])
        elif kind == 1:
            y = dsa_mixer(h, dsa_w_in[j], dsa_q_g[j], dsa_k_g[j], dsa_w_out[j])
        else:
            y = conv_mixer(h, conv_w_in[j], conv_dw_w[j], conv_dw_b[j], conv_ln_g[j],
                           conv_ln_b[j], conv_w_out[j])
        x = x + y
        x = x + conv_ffn(rms_norm(x, ffn_norm_g[i]), ffn_w_up[i], ffn_dw_w[i], ffn_w_down[i])
    return x
```

```python
import functools

import jax
import jax.numpy as jnp
from jax import lax
from jax.experimental import pallas as pl
from jax.experimental.pallas import tpu as pltpu

N_MIXERS = 3
EPS = 1e-6
M_HEADS = 8
M_DQK = 128
M_DV = 256
A_HEADS = 16
A_DHEAD = 128
IDX_HEADS = 16
IDX_DIM = 64
TOPK_MAX = 256
CONV_WIDTH = 31
FFN_CONV_WIDTH = 3

LANES = 128
SUBLANES = 8
VMEM_LIMIT = 52 * 1024 * 1024
MLSTM_CHUNK = 256
Q_BLOCK = 128
KEY_TILE = 256
CONV_HALO = 32
INT_MIN = -(2 ** 31)
NEG_BIG = -1e30

_BF = jnp.bfloat16
_F32 = jnp.float32


def _params(*sem):
    return pltpu.CompilerParams(dimension_semantics=sem, vmem_limit_bytes=VMEM_LIMIT)


def _rms_rows(x, g):
    ms = jnp.mean(x * x, axis=-1, keepdims=True)
    return x * lax.rsqrt(ms + EPS) * g


def _dot(a, b):
    return jnp.dot(a, b, preferred_element_type=_F32)


def _split3(x):
    hi = x.astype(_BF)
    r1 = x - hi.astype(_F32)
    mid = r1.astype(_BF)
    lo = (r1 - mid.astype(_F32)).astype(_BF)
    return hi, mid, lo


def _norm_matmul_kernel(x_ref, g_ref, w_ref, o_ref, xn_ref):
    @pl.when(pl.program_id(1) == 0)
    def _():
        xn_ref[...] = _rms_rows(x_ref[...], g_ref[...]).astype(_BF)

    o_ref[...] = _dot(xn_ref[...], w_ref[...]).astype(o_ref.dtype)


def _norm_matmul(x, g, w, *, tm, tn, name):
    n, d = x.shape
    m = w.shape[1]
    return pl.pallas_call(
        _norm_matmul_kernel,
        grid=(n // tm, m // tn),
        in_specs=[
            pl.BlockSpec((tm, d), lambda i, j: (i, 0)),
            pl.BlockSpec((1, d), lambda i, j: (0, 0)),
            pl.BlockSpec((d, tn), lambda i, j: (0, j)),
        ],
        out_specs=pl.BlockSpec((tm, tn), lambda i, j: (i, j)),
        out_shape=jax.ShapeDtypeStruct((n, m), _F32),
        scratch_shapes=[pltpu.VMEM((tm, d), _BF)],
        compiler_params=_params("parallel", "arbitrary"),
        name=name,
    )(x, g.reshape(1, d), w)


def _norm_glu_kernel(x_ref, g_ref, wa_ref, wg_ref, o_ref, xn_ref):
    @pl.when(pl.program_id(1) == 0)
    def _():
        xn_ref[...] = _rms_rows(x_ref[...], g_ref[...]).astype(_BF)

    xn = xn_ref[...]
    a = _dot(xn, wa_ref[...])
    gate = _dot(xn, wg_ref[...])
    o_ref[...] = a * jax.nn.sigmoid(gate)


def _norm_glu(x, g, w, *, tm, tn, name):
    n, d = x.shape
    half = w.shape[1] // 2
    nj = half // tn
    return pl.pallas_call(
        _norm_glu_kernel,
        grid=(n // tm, nj),
        in_specs=[
            pl.BlockSpec((tm, d), lambda i, j: (i, 0)),
            pl.BlockSpec((1, d), lambda i, j: (0, 0)),
            pl.BlockSpec((d, tn), lambda i, j: (0, j)),
            pl.BlockSpec((d, tn), lambda i, j: (0, j + nj)),
        ],
        out_specs=pl.BlockSpec((tm, tn), lambda i, j: (i, j)),
        out_shape=jax.ShapeDtypeStruct((n, half), _F32),
        scratch_shapes=[pltpu.VMEM((tm, d), _BF)],
        compiler_params=_params("parallel", "arbitrary"),
        name=name,
    )(x, g.reshape(1, d), w, w)


def _matmul_resid_kernel(a_ref, w_ref, r_ref, o_ref):
    o_ref[...] = r_ref[...] + _dot(a_ref[...], w_ref[...])


def _matmul_resid(a, w, resid, *, tm, tn, name):
    n, k = a.shape
    m = w.shape[1]
    return pl.pallas_call(
        _matmul_resid_kernel,
        grid=(n // tm, m // tn),
        in_specs=[
            pl.BlockSpec((tm, k), lambda i, j: (i, 0)),
            pl.BlockSpec((k, tn), lambda i, j: (0, j)),
            pl.BlockSpec((tm, tn), lambda i, j: (i, j)),
        ],
        out_specs=pl.BlockSpec((tm, tn), lambda i, j: (i, j)),
        out_shape=jax.ShapeDtypeStruct((n, m), _F32),
        compiler_params=_params("parallel", "arbitrary"),
        name=name,
    )(a, w, resid)


def _ffn_up_kernel(x_ref, xh_ref, g_ref, wa_ref, wb_ref, dwa_ref, dwb_ref, o_ref, xn_ref, *, tm, tiles_per_seq):
    i = pl.program_id(0)

    @pl.when(pl.program_id(1) == 0)
    def _():
        g = g_ref[...]
        halo = _rms_rows(xh_ref[...], g)
        halo = jnp.where(i % tiles_per_seq == 0, 0.0, halo)
        xn_ref[0:SUBLANES, :] = halo.astype(_BF)
        xn_ref[SUBLANES:, :] = _rms_rows(x_ref[...], g).astype(_BF)

    xn = xn_ref[...]

    def conv(h, dw):
        base = SUBLANES - (FFN_CONV_WIDTH - 1)
        out = dw[0:1, :] * h[base:base + tm, :]
        for k in range(1, FFN_CONV_WIDTH):
            out = out + dw[k:k + 1, :] * h[base + k:base + k + tm, :]
        return out

    a = conv(_dot(xn, wa_ref[...]), dwa_ref[...])
    b = conv(_dot(xn, wb_ref[...]), dwb_ref[...])
    o_ref[...] = (a * jax.nn.sigmoid(a) * b).astype(o_ref.dtype)


def _ffn_up(x, g, w_up, dw, *, seq, tm, tn, name):
    n, d = x.shape
    f = w_up.shape[1] // 2
    nj = f // tn
    hblk = tm // SUBLANES
    kern = functools.partial(_ffn_up_kernel, tm=tm, tiles_per_seq=seq // tm)
    return pl.pallas_call(
        kern,
        grid=(n // tm, nj),
        in_specs=[
            pl.BlockSpec((tm, d), lambda i, j: (i, 0)),
            pl.BlockSpec((SUBLANES, d), lambda i, j: (jnp.maximum(i * hblk - 1, 0), 0)),
            pl.BlockSpec((1, d), lambda i, j: (0, 0)),
            pl.BlockSpec((d, tn), lambda i, j: (0, j)),
            pl.BlockSpec((d, tn), lambda i, j: (0, j + nj)),
            pl.BlockSpec((FFN_CONV_WIDTH, tn), lambda i, j: (0, j)),
            pl.BlockSpec((FFN_CONV_WIDTH, tn), lambda i, j: (0, j + nj)),
        ],
        out_specs=pl.BlockSpec((tm, tn), lambda i, j: (i, j)),
        out_shape=jax.ShapeDtypeStruct((n, f), _BF),
        scratch_shapes=[pltpu.VMEM((tm + SUBLANES, d), _BF)],
        compiler_params=_params("parallel", "arbitrary"),
        name=name,
    )(x, x, g.reshape(1, d), w_up, w_up, dw, dw)


def _conv_mid_kernel(u_ref, uh_ref, dw_ref, db_ref, lg_ref, lb_ref, o_ref, ext_ref, c_ref, *, tm, tiles_per_seq, rb, cb):
    i = pl.program_id(0)
    d = u_ref.shape[1]
    ext_ref[0:CONV_HALO, :] = jnp.where(i % tiles_per_seq == 0, 0.0, uh_ref[...])
    ext_ref[CONV_HALO:, :] = u_ref[...]
    base = CONV_HALO - (CONV_WIDTH - 1)

    def col_block(c, carry):
        cs = pl.multiple_of(c * cb, cb)
        for r in range(tm // rb):
            acc = jnp.zeros((rb, cb), _F32) + db_ref[:, pl.ds(cs, cb)]
            for k in range(CONV_WIDTH):
                acc = acc + ext_ref[r * rb + base + k:r * rb + base + k + rb, pl.ds(cs, cb)] * dw_ref[k:k + 1, pl.ds(cs, cb)]
            c_ref[r * rb:(r + 1) * rb, pl.ds(cs, cb)] = acc
        return carry

    lax.fori_loop(0, d // cb, col_block, 0)
    c = c_ref[...]
    mu = jnp.mean(c, axis=-1, keepdims=True)
    xc = c - mu
    var = jnp.mean(xc * xc, axis=-1, keepdims=True)
    y = xc * lax.rsqrt(var + EPS) * lg_ref[...] + lb_ref[...]
    o_ref[...] = (y * jax.nn.sigmoid(y)).astype(o_ref.dtype)


def _conv_mid(u, dw, db, lg, lb, *, seq, tm, name):
    n, d = u.shape
    hblk = tm // CONV_HALO
    kern = functools.partial(_conv_mid_kernel, tm=tm, tiles_per_seq=seq // tm, rb=32, cb=512)
    return pl.pallas_call(
        kern,
        grid=(n // tm,),
        in_specs=[
            pl.BlockSpec((tm, d), lambda i: (i, 0)),
            pl.BlockSpec((CONV_HALO, d), lambda i: (jnp.maximum(i * hblk - 1, 0), 0)),
            pl.BlockSpec((CONV_WIDTH, d), lambda i: (0, 0)),
            pl.BlockSpec((1, d), lambda i: (0, 0)),
            pl.BlockSpec((1, d), lambda i: (0, 0)),
            pl.BlockSpec((1, d), lambda i: (0, 0)),
        ],
        out_specs=pl.BlockSpec((tm, d), lambda i: (i, 0)),
        out_shape=jax.ShapeDtypeStruct((n, d), _BF),
        scratch_shapes=[pltpu.VMEM((tm + CONV_HALO, d), _F32), pltpu.VMEM((tm, d), _F32)],
        compiler_params=_params("parallel"),
        name=name,
    )(u, u, dw, db.reshape(1, d), lg.reshape(1, d), lb.reshape(1, d))


def _gates_kernel(x_ref, g_ref, whi_ref, wlo_ref, b_ref, o_ref):
    xn = _rms_rows(x_ref[...], g_ref[...])
    xhi = xn.astype(_BF)
    xlo = (xn - xhi.astype(_F32)).astype(_BF)
    whi = whi_ref[...]
    z = _dot(xhi, whi) + _dot(xlo, whi) + _dot(xhi, wlo_ref[...]) + b_ref[...]
    log_f = jnp.minimum(z, 0.0) - jnp.log(1.0 + jnp.exp(-jnp.abs(z)))
    col = lax.broadcasted_iota(jnp.int32, z.shape, 1)
    o_ref[...] = jnp.where(col < M_HEADS, z, log_f)


def _gates(x, g, w_gate, bias, *, tm, name):
    n, d = x.shape
    ng = w_gate.shape[1]
    wp = jnp.zeros((d, LANES), _F32).at[:, :ng].set(w_gate)
    whi = wp.astype(_BF)
    wlo = (wp - whi.astype(_F32)).astype(_BF)
    bp = jnp.zeros((1, LANES), _F32).at[0, :ng].set(bias)
    out = pl.pallas_call(
        _gates_kernel,
        grid=(n // tm,),
        in_specs=[
            pl.BlockSpec((tm, d), lambda i: (i, 0)),
            pl.BlockSpec((1, d), lambda i: (0, 0)),
            pl.BlockSpec((d, LANES), lambda i: (0, 0)),
            pl.BlockSpec((d, LANES), lambda i: (0, 0)),
            pl.BlockSpec((1, LANES), lambda i: (0, 0)),
        ],
        out_specs=pl.BlockSpec((tm, LANES), lambda i: (i, 0)),
        out_shape=jax.ShapeDtypeStruct((n, LANES), _F32),
        compiler_params=_params("parallel"),
        name=name,
    )(x, g.reshape(1, d), whi, wlo, bp)
    return out[:, :ng]


def _mlstm_kernel(q_ref, k_ref, v_ref, o_ref, gc_ref, gr_ref, hg_ref, out_ref, c_ref, n_ref, m_ref, *, chunk):
    ln = chunk
    scale = M_DQK ** -0.5

    @pl.when(pl.program_id(1) == 0)
    def _():
        c_ref[...] = jnp.zeros_like(c_ref)
        n_ref[...] = jnp.zeros_like(n_ref)
        m_ref[...] = jnp.zeros_like(m_ref)

    gc = gc_ref[0]
    gr = gr_ref[0]
    rows = lax.broadcasted_iota(jnp.int32, (ln, ln), 0)
    cols = lax.broadcasted_iota(jnp.int32, (ln, ln), 1)
    causal = rows >= cols
    tril = causal.astype(_BF)
    triu = (rows <= cols).astype(_BF)
    cum_col = sum(_dot(tril, p) for p in _split3(gc))
    cum_row = sum(_dot(p, triu) for p in _split3(gr))

    for h in range(M_HEADS):
        f = M_HEADS + h
        bcol = cum_col[:, f:f + 1]
        brow = cum_row[f:f + 1, :]
        icol = gc[:, h:h + 1]
        irow = gr[h:h + 1, :]
        m_prev = m_ref[h, 0:1, 0:1]
        nrow = n_ref[h, 0:1, :]
        qh = q_ref[0, :, h * M_DQK:(h + 1) * M_DQK]
        kh = k_ref[0, :, h * M_DQK:(h + 1) * M_DQK]
        vb = v_ref[0, :, h * M_DV:(h + 1) * M_DV].astype(_BF)
        qb = qh.astype(_BF)

        dmat = jnp.where(causal, bcol - brow + irow, -jnp.inf)
        inter = bcol + m_prev
        m_t = jnp.maximum(inter, jnp.max(dmat, axis=1, keepdims=True))
        w_intra = jnp.exp(dmat - m_t)
        w_inter = jnp.exp(inter - m_t)
        s = lax.dot_general(qb, kh.astype(_BF), (((1,), (1,)), ((), ())), preferred_element_type=_F32)
        s = s * (w_intra * scale)
        qc = _dot(qb, c_ref[h].astype(_BF)) * scale
        num = w_inter * qc + _dot(s.astype(_BF), vb)
        qn = jnp.sum(qh * nrow, axis=1, keepdims=True) * scale
        den = w_inter * qn + jnp.sum(s, axis=1, keepdims=True)
        hh = num / jnp.maximum(jnp.abs(den), jnp.exp(-m_t))

        bl = bcol[ln - 1:ln, :]
        g_col = bl - bcol + icol
        g_row = bl - brow + irow
        m_new = jnp.maximum(bl + m_prev, jnp.max(g_row, axis=1, keepdims=True))
        a_prev = jnp.exp(bl + m_prev - m_new)
        kw = kh * jnp.exp(g_col - m_new)
        upd = lax.dot_general(kw.astype(_BF), vb, (((0,), (0,)), ((), ())), preferred_element_type=_F32)
        c_ref[h] = a_prev * c_ref[h] + upd
        n_new = a_prev * nrow + jnp.sum(kw, axis=0, keepdims=True)
        n_ref[h] = jnp.broadcast_to(n_new, (SUBLANES, M_DQK))
        m_ref[h] = jnp.broadcast_to(m_new, (SUBLANES, LANES))

        hn = hh * lax.rsqrt(jnp.mean(hh * hh, axis=1, keepdims=True) + EPS) * hg_ref[:, h * M_DV:(h + 1) * M_DV]
        og = o_ref[0, :, h * M_DV:(h + 1) * M_DV]
        out_ref[0, :, h * M_DV:(h + 1) * M_DV] = (hn * jax.nn.sigmoid(og)).astype(out_ref.dtype)


def _mlstm_core(proj, gates_col, gates_row, head_g, *, name):
    b, t, _ = proj.shape
    hk = M_HEADS * M_DQK
    hv = M_HEADS * M_DV
    ln = MLSTM_CHUNK
    ng = 2 * M_HEADS
    kern = functools.partial(_mlstm_kernel, chunk=ln)
    return pl.pallas_call(
        kern,
        grid=(b, t // ln),
        in_specs=[
            pl.BlockSpec((1, ln, hk), lambda i, c: (i, c, 0)),
            pl.BlockSpec((1, ln, hk), lambda i, c: (i, c, 1)),
            pl.BlockSpec((1, ln, hv), lambda i, c: (i, c, (2 * hk) // hv)),
            pl.BlockSpec((1, ln, hv), lambda i, c: (i, c, (2 * hk) // hv + 1)),
            pl.BlockSpec((1, ln, ng), lambda i, c: (i, c, 0)),
            pl.BlockSpec((1, ng, ln), lambda i, c: (i, 0, c)),
            pl.BlockSpec((1, hv), lambda i, c: (0, 0)),
        ],
        out_specs=pl.BlockSpec((1, ln, hv), lambda i, c: (i, c, 0)),
        out_shape=jax.ShapeDtypeStruct((b, t, hv), _BF),
        scratch_shapes=[
            pltpu.VMEM((M_HEADS, M_DQK, M_DV), _F32),
            pltpu.VMEM((M_HEADS, SUBLANES, M_DQK), _F32),
            pltpu.VMEM((M_HEADS, SUBLANES, LANES), _F32),
        ],
        compiler_params=_params("parallel", "arbitrary"),
        name=name,
    )(proj, proj, proj, proj, gates_col, gates_row, head_g.reshape(1, hv))


def _dsa_prep_kernel(q_ref, k_ref, qg_ref, kg_ref, qo_ref, ko_ref):
    qg = qg_ref[...]
    scale = A_DHEAD ** -0.5
    for h in range(A_HEADS):
        sl = slice(h * A_DHEAD, (h + 1) * A_DHEAD)
        qo_ref[:, sl] = (_rms_rows(q_ref[:, sl], qg) * scale).astype(_BF)
    ko_ref[...] = _rms_rows(k_ref[...], kg_ref[...]).astype(_BF)


def _dsa_prep(proj, q_g, k_g, *, tm, name):
    n = proj.shape[0]
    qd = A_HEADS * A_DHEAD
    return pl.pallas_call(
        _dsa_prep_kernel,
        grid=(n // tm,),
        in_specs=[
            pl.BlockSpec((tm, qd), lambda i: (i, 0)),
            pl.BlockSpec((tm, A_DHEAD), lambda i: (i, qd // A_DHEAD)),
            pl.BlockSpec((1, A_DHEAD), lambda i: (0, 0)),
            pl.BlockSpec((1, A_DHEAD), lambda i: (0, 0)),
        ],
        out_specs=[
            pl.BlockSpec((tm, qd), lambda i: (i, 0)),
            pl.BlockSpec((tm, A_DHEAD), lambda i: (i, 0)),
        ],
        out_shape=[jax.ShapeDtypeStruct((n, qd), _BF), jax.ShapeDtypeStruct((n, A_DHEAD), _BF)],
        compiler_params=_params("parallel"),
        name=name,
    )(proj, proj, q_g.reshape(1, A_DHEAD), k_g.reshape(1, A_DHEAD))


def _dsa_kernel(ki_ref, k_ref, vt_ref, qit_ref, wi_ref, qt_ref, o_ref,
                key_ref, p_ref, acc_ref, m_ref, l_ref, alpha_ref, cut_ref, *, tk, topk):
    q0 = pl.program_id(1) * Q_BLOCK
    n_kt = (q0 + Q_BLOCK - 1) // tk + 1
    n_bits_idx = (key_ref.shape[0] - 1).bit_length()
    lane_q = q0 + lax.broadcasted_iota(jnp.int32, (tk, Q_BLOCK), 1)
    row = lax.broadcasted_iota(jnp.int32, (tk, Q_BLOCK), 0)

    qit = qit_ref[0, 0]
    wi = wi_ref[0, 0] * ((IDX_HEADS * IDX_DIM) ** -0.5)

    def score_tile(kt, carry):
        start = pl.multiple_of(kt * tk, tk)
        lg = _dot(ki_ref[0, pl.ds(start, tk), :], qit)
        w = jnp.maximum(lg, 0.0) * wi
        s = w[:, 0:Q_BLOCK]
        for h in range(1, IDX_HEADS):
            s = s + w[:, h * Q_BLOCK:(h + 1) * Q_BLOCK]
        bits = pltpu.bitcast(s, jnp.int32)
        key = bits ^ ((bits >> 31) & 0x7FFFFFFF)
        key = jnp.where(bits == INT_MIN, 0, key)
        key = jnp.where(start + row <= lane_q, key, INT_MIN)
        key_ref[pl.ds(start, tk), :] = key
        return carry

    lax.fori_loop(0, n_kt, score_tile, 0)

    def count(pred):
        def body(kt, acc):
            start = pl.multiple_of(kt * tk, tk)
            hit = pred(key_ref[pl.ds(start, tk), :], start + row).astype(jnp.int32)
            return acc + jnp.sum(hit.reshape(tk // SUBLANES, SUBLANES, Q_BLOCK), axis=0)
        acc = lax.fori_loop(0, n_kt, body, jnp.zeros((SUBLANES, Q_BLOCK), jnp.int32))
        return jnp.sum(acc, axis=0, keepdims=True)

    def thr_bit(b, t_u):
        cand_u = t_u | lax.shift_left(jnp.int32(1), 31 - b)
        cand = cand_u ^ INT_MIN
        cnt = count(lambda keys, idx: keys >= cand)
        return jnp.where(cnt >= topk, cand_u, t_u)

    thr = lax.fori_loop(0, 32, thr_bit, jnp.zeros((1, Q_BLOCK), jnp.int32)) ^ INT_MIN
    n_gt = count(lambda keys, idx: keys > thr)
    n_eq = count(lambda keys, idx: keys == thr)
    need = topk - n_gt
    cut_ref[...] = jnp.full((1, Q_BLOCK), 2 ** 30, jnp.int32)
    tied = jnp.logical_and(n_eq != need, thr != INT_MIN)

    @pl.when(jnp.max(tied.astype(jnp.int32)) > 0)
    def _():
        def cut_bit(b, p):
            cand = p | lax.shift_left(jnp.int32(1), n_bits_idx - 1 - b)
            cnt = count(lambda keys, idx: jnp.logical_and(keys == thr, idx < cand))
            return jnp.where(cnt < need, cand, p)
        cut_ref[...] = lax.fori_loop(0, n_bits_idx, cut_bit, jnp.zeros((1, Q_BLOCK), jnp.int32))

    cut = cut_ref[...]

    m_ref[...] = jnp.full(m_ref.shape, NEG_BIG, _F32)
    l_ref[...] = jnp.zeros(l_ref.shape, _F32)
    acc_ref[...] = jnp.zeros(acc_ref.shape, _F32)
    qt = qt_ref[0, 0]

    def att_tile(kt, carry):
        start = pl.multiple_of(kt * tk, tk)
        keys = key_ref[pl.ds(start, tk), :]
        sel = jnp.logical_or(keys > thr, jnp.logical_and(keys == thr, start + row <= cut))
        sel = jnp.logical_and(sel, keys != INT_MIN)
        lg = _dot(k_ref[0, pl.ds(start, tk), :], qt)
        for h in range(A_HEADS):
            sl = slice(h * Q_BLOCK, (h + 1) * Q_BLOCK)
            x = jnp.where(sel, lg[:, sl], NEG_BIG)
            m_old = m_ref[:, sl]
            m_new = jnp.maximum(m_old, jnp.max(x, axis=0, keepdims=True))
            p = jnp.exp(x - m_new)
            alpha = jnp.exp(m_old - m_new)
            l_ref[:, sl] = alpha * l_ref[:, sl] + jnp.sum(p, axis=0, keepdims=True)
            m_ref[:, sl] = m_new
            alpha_ref[:, sl] = alpha
            p_ref[:, sl] = p.astype(_BF)
        pv = _dot(vt_ref[0, :, pl.ds(start, tk)], p_ref[...])
        acc_ref[...] = acc_ref[...] * alpha_ref[...] + pv
        return carry

    lax.fori_loop(0, n_kt, att_tile, 0)
    o_ref[0, 0] = (acc_ref[...] / l_ref[...]).astype(o_ref.dtype)


def _dsa_attend(ki, k, vt, qit, wi, qt, *, name):
    b, t, _ = ki.shape
    nb = t // Q_BLOCK
    hq = A_HEADS * Q_BLOCK
    topk = min(TOPK_MAX, t // 4)
    kern = functools.partial(_dsa_kernel, tk=KEY_TILE, topk=topk)
    return pl.pallas_call(
        kern,
        grid=(b, nb),
        in_specs=[
            pl.BlockSpec((1, t, IDX_DIM), lambda i, j: (i, 0, 0)),
            pl.BlockSpec((1, t, A_DHEAD), lambda i, j: (i, 0, 0)),
            pl.BlockSpec((1, A_DHEAD, t), lambda i, j: (i, 0, 0)),
            pl.BlockSpec((1, 1, IDX_DIM, hq), lambda i, j: (i, j, 0, 0)),
            pl.BlockSpec((1, 1, 1, hq), lambda i, j: (i, j, 0, 0)),
            pl.BlockSpec((1, 1, A_DHEAD, hq), lambda i, j: (i, j, 0, 0)),
        ],
        out_specs=pl.BlockSpec((1, 1, A_DHEAD, hq), lambda i, j: (i, j, 0, 0)),
        out_shape=jax.ShapeDtypeStruct((b, nb, A_DHEAD, hq), _BF),
        scratch_shapes=[
            pltpu.VMEM((t, Q_BLOCK), jnp.int32),
            pltpu.VMEM((KEY_TILE, hq), _BF),
            pltpu.VMEM((A_DHEAD, hq), _F32),
            pltpu.VMEM((1, hq), _F32),
            pltpu.VMEM((1, hq), _F32),
            pltpu.VMEM((1, hq), _F32),
            pltpu.VMEM((1, Q_BLOCK), jnp.int32),
        ],
        compiler_params=_params("parallel", "arbitrary"),
        name=name,
    )(ki, k, vt, qit, wi, qt)


def _mlstm_layer(x2, norm_g, w_in, gate_b, head_g, w_out, *, batch, seq):
    n, d = x2.shape
    hk = M_HEADS * M_DQK
    hv = M_HEADS * M_DV
    main = 2 * hk + 2 * hv
    proj = _norm_matmul(x2, norm_g, w_in[:, :main].astype(_BF), tm=512, tn=1024, name="mlstm_in")
    gates = _gates(x2, norm_g, w_in[:, main:], gate_b, tm=512, name="mlstm_gates")
    gates = gates.reshape(batch, seq, 2 * M_HEADS)
    h = _mlstm_core(proj.reshape(batch, seq, main), gates, gates.transpose(0, 2, 1), head_g, name="mlstm_core")
    return _matmul_resid(h.reshape(n, hv), w_out.astype(_BF), x2, tm=512, tn=1024, name="mlstm_out")


def _dsa_layer(x2, norm_g, w_in, q_g, k_g, w_out, *, batch, seq):
    n, d = x2.shape
    qd = A_HEADS * A_DHEAD
    qid = IDX_HEADS * IDX_DIM
    o_k, o_v, o_qi = qd, qd + A_DHEAD, qd + 2 * A_DHEAD
    o_ki, o_wi, o_end = o_qi + qid, o_qi + qid + IDX_DIM, o_qi + qid + IDX_DIM + IDX_HEADS
    tn = 512
    padded = -(-o_end // tn) * tn
    w_pad = jnp.zeros((d, padded), _BF).at[:, :o_end].set(w_in.astype(_BF))
    proj = _norm_matmul(x2, norm_g, w_pad, tm=512, tn=tn, name="dsa_in")
    qn, kn = _dsa_prep(proj, q_g, k_g, tm=512, name="dsa_qknorm")
    nb = seq // Q_BLOCK
    qt = qn.reshape(batch, nb, Q_BLOCK, A_HEADS, A_DHEAD).transpose(0, 1, 4, 3, 2).reshape(batch, nb, A_DHEAD, A_HEADS * Q_BLOCK)
    qit = proj[:, o_qi:o_ki].astype(_BF).reshape(batch, nb, Q_BLOCK, IDX_HEADS, IDX_DIM)
    qit = qit.transpose(0, 1, 4, 3, 2).reshape(batch, nb, IDX_DIM, IDX_HEADS * Q_BLOCK)
    wi = proj[:, o_wi:o_end].reshape(batch, nb, Q_BLOCK, IDX_HEADS).transpose(0, 1, 3, 2).reshape(batch, nb, 1, IDX_HEADS * Q_BLOCK)
    ki = proj[:, o_ki:o_wi].astype(_BF).reshape(batch, seq, IDX_DIM)
    vt = proj[:, o_v:o_qi].astype(_BF).reshape(batch, seq, A_DHEAD).transpose(0, 2, 1)
    out_t = _dsa_attend(ki, kn.reshape(batch, seq, A_DHEAD), vt, qit, wi, qt, name="dsa_attend")
    out = out_t.reshape(batch, nb, A_DHEAD, A_HEADS, Q_BLOCK).transpose(0, 1, 4, 3, 2).reshape(n, qd)
    return _matmul_resid(out, w_out.astype(_BF), x2, tm=512, tn=1024, name="dsa_out")


def _conv_layer(x2, norm_g, w_in, dw_w, dw_b, ln_g, ln_b, w_out, *, seq):
    u = _norm_glu(x2, norm_g, w_in.astype(_BF), tm=512, tn=512, name="conv_in")
    v = _conv_mid(u, dw_w, dw_b, ln_g, ln_b, seq=seq, tm=256, name="conv_mid")
    return _matmul_resid(v, w_out.astype(_BF), x2, tm=512, tn=1024, name="conv_out")


def _ffn_layer(x2, norm_g, w_up, dw, w_down, *, seq):
    act = _ffn_up(x2, norm_g, w_up.astype(_BF), dw, seq=seq, tm=512, tn=512, name="ffn_up")
    return _matmul_resid(act, w_down.astype(_BF), x2, tm=512, tn=512, name="ffn_down")


def kernel(x, mix_norm_g, ffn_norm_g, mlstm_w_in, mlstm_gate_b, mlstm_head_g, mlstm_w_out,
           dsa_w_in, dsa_q_g, dsa_k_g, dsa_w_out, conv_w_in, conv_dw_w, conv_dw_b,
           conv_ln_g, conv_ln_b, conv_w_out, ffn_w_up, ffn_dw_w, ffn_w_down):
    batch, seq, d = x.shape
    depth = mix_norm_g.shape[0]
    x2 = x.reshape(batch * seq, d)
    for i in range(depth):
        kind = i % N_MIXERS
        j = i // N_MIXERS
        if kind == 0:
            x2 = _mlstm_layer(x2, mix_norm_g[i], mlstm_w_in[j], mlstm_gate_b[j], mlstm_head_g[j], mlstm_w_out[j],
                              batch=batch, seq=seq)
        elif kind == 1:
            x2 = _dsa_layer(x2, mix_norm_g[i], dsa_w_in[j], dsa_q_g[j], dsa_k_g[j], dsa_w_out[j], batch=batch, seq=seq)
        else:
            x2 = _conv_layer(x2, mix_norm_g[i], conv_w_in[j], conv_dw_w[j], conv_dw_b[j], conv_ln_g[j], conv_ln_b[j],
                             conv_w_out[j], seq=seq)
        x2 = _ffn_layer(x2, ffn_norm_g[i], ffn_w_up[i], ffn_dw_w[i], ffn_w_down[i], seq=seq)
    return x2.reshape(batch, seq, d)
```

```python
import functools

import jax
import jax.numpy as jnp
from jax import lax
from jax.experimental import pallas as pl
from jax.experimental.pallas import tpu as pltpu

N_MIXERS = 3
EPS = 1e-6
M_HEADS = 8
M_DQK = 128
M_DV = 256
A_HEADS = 16
A_DHEAD = 128
IDX_HEADS = 16
IDX_DIM = 64
TOPK_MAX = 256
CONV_WIDTH = 31
FFN_CONV_WIDTH = 3

LANES = 128
SUBLANES = 8
FFN_HALO = 16
VMEM_LIMIT = 52 * 1024 * 1024
MLSTM_CHUNK = 256
Q_BLOCK = 128
KEY_TILE = 256
CONV_HALO = 32
LOG2_E = 1.4426950408889634
INT_MIN = -(2 ** 31)
NEG_BIG = -1e30

_BF = jnp.bfloat16
_F32 = jnp.float32


def _params(*sem):
    return pltpu.CompilerParams(dimension_semantics=sem, vmem_limit_bytes=VMEM_LIMIT)


def _rms_rows(x, g):
    ms = jnp.mean(x * x, axis=-1, keepdims=True)
    return x * lax.rsqrt(ms + EPS) * g


def _dot(a, b):
    return jnp.dot(a, b, preferred_element_type=_F32)


def _split3(x):
    hi = x.astype(_BF)
    r1 = x - hi.astype(_F32)
    mid = r1.astype(_BF)
    lo = (r1 - mid.astype(_F32)).astype(_BF)
    return hi, mid, lo


def _norm_matmul_kernel(x_ref, g_ref, w_ref, o_ref, xn_ref):
    @pl.when(pl.program_id(1) == 0)
    def _():
        xn_ref[...] = _rms_rows(x_ref[...], g_ref[...]).astype(_BF)

    o_ref[...] = _dot(xn_ref[...], w_ref[...]).astype(o_ref.dtype)


def _norm_matmul(x, g, w, *, tm, tn, name):
    n, d = x.shape
    m = w.shape[1]
    return pl.pallas_call(
        _norm_matmul_kernel,
        grid=(n // tm, m // tn),
        in_specs=[
            pl.BlockSpec((tm, d), lambda i, j: (i, 0)),
            pl.BlockSpec((1, d), lambda i, j: (0, 0)),
            pl.BlockSpec((d, tn), lambda i, j: (0, j)),
        ],
        out_specs=pl.BlockSpec((tm, tn), lambda i, j: (i, j)),
        out_shape=jax.ShapeDtypeStruct((n, m), _F32),
        scratch_shapes=[pltpu.VMEM((tm, d), _BF)],
        compiler_params=_params("parallel", "arbitrary"),
        name=name,
    )(x, g.reshape(1, d), w)


def _norm_glu_kernel(x_ref, g_ref, wa_ref, wg_ref, o_ref, xn_ref):
    @pl.when(pl.program_id(1) == 0)
    def _():
        xn_ref[...] = _rms_rows(x_ref[...], g_ref[...]).astype(_BF)

    xn = xn_ref[...]
    a = _dot(xn, wa_ref[...])
    gate = _dot(xn, wg_ref[...])
    o_ref[...] = a * jax.nn.sigmoid(gate)


def _norm_glu(x, g, w, *, tm, tn, name):
    n, d = x.shape
    half = w.shape[1] // 2
    nj = half // tn
    return pl.pallas_call(
        _norm_glu_kernel,
        grid=(n // tm, nj),
        in_specs=[
            pl.BlockSpec((tm, d), lambda i, j: (i, 0)),
            pl.BlockSpec((1, d), lambda i, j: (0, 0)),
            pl.BlockSpec((d, tn), lambda i, j: (0, j)),
            pl.BlockSpec((d, tn), lambda i, j: (0, j + nj)),
        ],
        out_specs=pl.BlockSpec((tm, tn), lambda i, j: (i, j)),
        out_shape=jax.ShapeDtypeStruct((n, half), _F32),
        scratch_shapes=[pltpu.VMEM((tm, d), _BF)],
        compiler_params=_params("parallel", "arbitrary"),
        name=name,
    )(x, g.reshape(1, d), w, w)


def _matmul_resid_kernel(a_ref, w_ref, r_ref, o_ref):
    o_ref[...] = r_ref[...] + _dot(a_ref[...], w_ref[...])


def _matmul_resid(a, w, resid, *, tm, tn, name):
    n, k = a.shape
    m = w.shape[1]
    return pl.pallas_call(
        _matmul_resid_kernel,
        grid=(n // tm, m // tn),
        in_specs=[
            pl.BlockSpec((tm, k), lambda i, j: (i, 0)),
            pl.BlockSpec((k, tn), lambda i, j: (0, j)),
            pl.BlockSpec((tm, tn), lambda i, j: (i, j)),
        ],
        out_specs=pl.BlockSpec((tm, tn), lambda i, j: (i, j)),
        out_shape=jax.ShapeDtypeStruct((n, m), _F32),
        compiler_params=_params("parallel", "arbitrary"),
        name=name,
    )(a, w, resid)


def _ffn_kernel(x_ref, xh_ref, g_ref, wa_ref, wb_ref, dwa_ref, dwb_ref, wd_ref, o_ref,
                xn_ref, ha0_ref, hb0_ref, ha1_ref, hb1_ref, act_ref, *, tm, nj, total, tiles_per_seq, act_rows):
    s = pl.program_id(0)

    @pl.when(s == 0)
    def _():
        ha1_ref[...] = jnp.zeros_like(ha1_ref)
        hb1_ref[...] = jnp.zeros_like(hb1_ref)
        o_ref[...] = jnp.zeros_like(o_ref)

    @pl.when(jnp.logical_and(s % nj == 0, s < total))
    def _():
        g = g_ref[...]
        halo = _rms_rows(xh_ref[...], g)
        halo = jnp.where((s // nj) % tiles_per_seq == 0, 0.0, halo)
        xn_ref[0:FFN_HALO, :] = halo.astype(_BF)
        xn_ref[FFN_HALO:, :] = _rms_rows(x_ref[...], g).astype(_BF)

    @pl.when((s - 1) % nj == 0)
    def _():
        o_ref[...] = x_ref[...]

    def conv(h_ref, dw, r0, rows):
        base = r0 + FFN_HALO - (FFN_CONV_WIDTH - 1)
        out = dw[0:1, :] * h_ref[base:base + rows, :]
        for k in range(1, FFN_CONV_WIDTH):
            out = out + dw[k:k + 1, :] * h_ref[base + k:base + k + rows, :]
        return out

    def step(ha_prev, hb_prev, ha_next, hb_next):
        ha_next[...] = _dot(xn_ref[...], wa_ref[...])
        hb_next[...] = _dot(xn_ref[...], wb_ref[...])
        dwa = dwa_ref[...]
        dwb = dwb_ref[...]
        for r0 in range(0, tm, act_rows):
            a = conv(ha_prev, dwa, r0, act_rows)
            b = conv(hb_prev, dwb, r0, act_rows)
            act_ref[r0:r0 + act_rows, :] = (a * jax.nn.sigmoid(a) * b).astype(_BF)
        o_ref[...] += _dot(act_ref[...], wd_ref[...])

    @pl.when(s % 2 == 0)
    def _():
        step(ha1_ref, hb1_ref, ha0_ref, hb0_ref)

    @pl.when(s % 2 == 1)
    def _():
        step(ha0_ref, hb0_ref, ha1_ref, hb1_ref)


def _ffn(x, g, w_up, dw, w_down, *, seq, tm, tn, name):
    n, d = x.shape
    f = w_up.shape[1] // 2
    nj = f // tn
    total = (n // tm) * nj
    hblk = tm // FFN_HALO
    kern = functools.partial(_ffn_kernel, tm=tm, nj=nj, total=total, tiles_per_seq=seq // tm, act_rows=64)

    def cur(s):
        c = jnp.minimum(s, total - 1)
        return c // nj, c % nj

    def prev(s):
        p = jnp.maximum(s - 1, 0)
        return p // nj, p % nj

    h_scratch = pltpu.VMEM((tm + FFN_HALO, tn), _F32)
    return pl.pallas_call(
        kern,
        grid=(total + 1,),
        in_specs=[
            pl.BlockSpec((tm, d), lambda s: (cur(s)[0], 0)),
            pl.BlockSpec((FFN_HALO, d), lambda s: (jnp.maximum(cur(s)[0] * hblk - 1, 0), 0)),
            pl.BlockSpec((1, d), lambda s: (0, 0)),
            pl.BlockSpec((d, tn), lambda s: (0, cur(s)[1])),
            pl.BlockSpec((d, tn), lambda s: (0, cur(s)[1] + nj)),
            pl.BlockSpec((FFN_CONV_WIDTH, tn), lambda s: (0, prev(s)[1])),
            pl.BlockSpec((FFN_CONV_WIDTH, tn), lambda s: (0, prev(s)[1] + nj)),
            pl.BlockSpec((tn, d), lambda s: (prev(s)[1], 0)),
        ],
        out_specs=pl.BlockSpec((tm, d), lambda s: (prev(s)[0], 0)),
        out_shape=jax.ShapeDtypeStruct((n, d), _F32),
        scratch_shapes=[pltpu.VMEM((tm + FFN_HALO, d), _BF), h_scratch, h_scratch, h_scratch, h_scratch,
                        pltpu.VMEM((tm, tn), _BF)],
        compiler_params=_params("arbitrary"),
        name=name,
    )(x, x, g.reshape(1, d), w_up, w_up, dw, dw, w_down)


def _conv_mid_kernel(u_ref, uh_ref, dw_ref, db_ref, lg_ref, lb_ref, o_ref, ext_ref, c_ref, sh_ref, *, tm, tiles_per_seq, rb, cb):
    i = pl.program_id(0)
    d = u_ref.shape[1]
    ext_ref[0:CONV_HALO, :] = jnp.where(i % tiles_per_seq == 0, 0.0, uh_ref[...])
    ext_ref[CONV_HALO:, :] = u_ref[...]
    base = CONV_HALO - (CONV_WIDTH - 1)

    def col_block(c, carry):
        cs = pl.multiple_of(c * cb, cb)
        for r in range(tm // rb):
            acc = jnp.zeros((rb, cb), _F32) + db_ref[:, pl.ds(cs, cb)]
            for phase in range(SUBLANES):
                taps = [k for k in range(CONV_WIDTH) if (base + k) % SUBLANES == phase]
                span = rb + max(base + k - phase for k in taps)
                sh_ref[0:span, :] = ext_ref[r * rb + phase:r * rb + phase + span, pl.ds(cs, cb)]
                for k in taps:
                    off = base + k - phase
                    acc = acc + sh_ref[off:off + rb, :] * dw_ref[k:k + 1, pl.ds(cs, cb)]
            c_ref[r * rb:(r + 1) * rb, pl.ds(cs, cb)] = acc
        return carry

    lax.fori_loop(0, d // cb, col_block, 0)
    c = c_ref[...]
    mu = jnp.mean(c, axis=-1, keepdims=True)
    xc = c - mu
    var = jnp.mean(xc * xc, axis=-1, keepdims=True)
    y = xc * lax.rsqrt(var + EPS) * lg_ref[...] + lb_ref[...]
    o_ref[...] = (y * jax.nn.sigmoid(y)).astype(o_ref.dtype)


def _conv_mid(u, dw, db, lg, lb, *, seq, tm, name):
    n, d = u.shape
    hblk = tm // CONV_HALO
    rb, cb = 64, 256
    kern = functools.partial(_conv_mid_kernel, tm=tm, tiles_per_seq=seq // tm, rb=rb, cb=cb)
    return pl.pallas_call(
        kern,
        grid=(n // tm,),
        in_specs=[
            pl.BlockSpec((tm, d), lambda i: (i, 0)),
            pl.BlockSpec((CONV_HALO, d), lambda i: (jnp.maximum(i * hblk - 1, 0), 0)),
            pl.BlockSpec((CONV_WIDTH, d), lambda i: (0, 0)),
            pl.BlockSpec((1, d), lambda i: (0, 0)),
            pl.BlockSpec((1, d), lambda i: (0, 0)),
            pl.BlockSpec((1, d), lambda i: (0, 0)),
        ],
        out_specs=pl.BlockSpec((tm, d), lambda i: (i, 0)),
        out_shape=jax.ShapeDtypeStruct((n, d), _BF),
        scratch_shapes=[pltpu.VMEM((tm + CONV_HALO, d), _F32), pltpu.VMEM((tm, d), _F32),
                        pltpu.VMEM((rb + CONV_HALO, cb), _F32)],
        compiler_params=_params("parallel"),
        name=name,
    )(u, u, dw, db.reshape(1, d), lg.reshape(1, d), lb.reshape(1, d))


def _gates_kernel(x_ref, g_ref, whi_ref, wlo_ref, b_ref, o_ref):
    xn = _rms_rows(x_ref[...], g_ref[...])
    xhi = xn.astype(_BF)
    xlo = (xn - xhi.astype(_F32)).astype(_BF)
    whi = whi_ref[...]
    z = _dot(xhi, whi) + _dot(xlo, whi) + _dot(xhi, wlo_ref[...]) + b_ref[...]
    log_f = jnp.minimum(z, 0.0) - jnp.log(1.0 + jnp.exp(-jnp.abs(z)))
    col = lax.broadcasted_iota(jnp.int32, z.shape, 1)
    o_ref[...] = jnp.where(col < M_HEADS, z, log_f)


def _gates(x, g, w_gate, bias, *, tm, name):
    n, d = x.shape
    ng = w_gate.shape[1]
    wp = jnp.zeros((d, LANES), _F32).at[:, :ng].set(w_gate)
    whi = wp.astype(_BF)
    wlo = (wp - whi.astype(_F32)).astype(_BF)
    bp = jnp.zeros((1, LANES), _F32).at[0, :ng].set(bias)
    out = pl.pallas_call(
        _gates_kernel,
        grid=(n // tm,),
        in_specs=[
            pl.BlockSpec((tm, d), lambda i: (i, 0)),
            pl.BlockSpec((1, d), lambda i: (0, 0)),
            pl.BlockSpec((d, LANES), lambda i: (0, 0)),
            pl.BlockSpec((d, LANES), lambda i: (0, 0)),
            pl.BlockSpec((1, LANES), lambda i: (0, 0)),
        ],
        out_specs=pl.BlockSpec((tm, LANES), lambda i: (i, 0)),
        out_shape=jax.ShapeDtypeStruct((n, LANES), _F32),
        compiler_params=_params("parallel"),
        name=name,
    )(x, g.reshape(1, d), whi, wlo, bp)
    return out[:, :ng]


def _mlstm_kernel(q_ref, k_ref, v_ref, o_ref, gc_ref, gr_ref, hg_ref, out_ref, c_ref, n_ref, m_ref, *, chunk):
    ln = chunk
    scale = M_DQK ** -0.5

    @pl.when(pl.program_id(1) == 0)
    def _():
        c_ref[...] = jnp.zeros_like(c_ref)
        n_ref[...] = jnp.zeros_like(n_ref)
        m_ref[...] = jnp.zeros_like(m_ref)

    gc = gc_ref[0]
    gr = gr_ref[0]
    rows = lax.broadcasted_iota(jnp.int32, (ln, ln), 0)
    cols = lax.broadcasted_iota(jnp.int32, (ln, ln), 1)
    causal = rows >= cols
    tril = causal.astype(_BF)
    triu = (rows <= cols).astype(_BF)
    cum_col = sum(_dot(tril, p) for p in _split3(gc))
    cum_row = sum(_dot(p, triu) for p in _split3(gr))

    for h in range(M_HEADS):
        f = M_HEADS + h
        bcol = cum_col[:, f:f + 1]
        brow = cum_row[f:f + 1, :]
        icol = gc[:, h:h + 1]
        irow = gr[h:h + 1, :]
        m_prev = m_ref[h, 0:1, 0:1]
        nrow = n_ref[h, 0:1, :]
        qh = q_ref[0, :, h * M_DQK:(h + 1) * M_DQK]
        kh = k_ref[0, :, h * M_DQK:(h + 1) * M_DQK]
        vb = v_ref[0, :, h * M_DV:(h + 1) * M_DV].astype(_BF)
        qb = qh.astype(_BF)

        dmat = jnp.where(causal, bcol - brow + irow, -jnp.inf)
        inter = bcol + m_prev
        m_t = jnp.maximum(inter, jnp.max(dmat, axis=1, keepdims=True))
        w_intra = jnp.exp(dmat - m_t)
        w_inter = jnp.exp(inter - m_t)
        s = lax.dot_general(qb, kh.astype(_BF), (((1,), (1,)), ((), ())), preferred_element_type=_F32)
        s = s * (w_intra * scale)
        qc = _dot(qb, c_ref[h].astype(_BF)) * scale
        num = w_inter * qc + _dot(s.astype(_BF), vb)
        qn = jnp.sum(qh * nrow, axis=1, keepdims=True) * scale
        den = w_inter * qn + jnp.sum(s, axis=1, keepdims=True)
        hh = num / jnp.maximum(jnp.abs(den), jnp.exp(-m_t))

        bl = bcol[ln - 1:ln, :]
        g_col = bl - bcol + icol
        g_row = bl - brow + irow
        m_new = jnp.maximum(bl + m_prev, jnp.max(g_row, axis=1, keepdims=True))
        a_prev = jnp.exp(bl + m_prev - m_new)
        kw = kh * jnp.exp(g_col - m_new)
        upd = lax.dot_general(kw.astype(_BF), vb, (((0,), (0,)), ((), ())), preferred_element_type=_F32)
        c_ref[h] = a_prev * c_ref[h] + upd
        n_new = a_prev * nrow + jnp.sum(kw, axis=0, keepdims=True)
        n_ref[h] = jnp.broadcast_to(n_new, (SUBLANES, M_DQK))
        m_ref[h] = jnp.broadcast_to(m_new, (SUBLANES, LANES))

        hn = hh * lax.rsqrt(jnp.mean(hh * hh, axis=1, keepdims=True) + EPS) * hg_ref[:, h * M_DV:(h + 1) * M_DV]
        og = o_ref[0, :, h * M_DV:(h + 1) * M_DV]
        out_ref[0, :, h * M_DV:(h + 1) * M_DV] = (hn * jax.nn.sigmoid(og)).astype(out_ref.dtype)


def _mlstm_core(proj, gates_col, gates_row, head_g, *, name):
    b, t, _ = proj.shape
    hk = M_HEADS * M_DQK
    hv = M_HEADS * M_DV
    ln = MLSTM_CHUNK
    ng = 2 * M_HEADS
    kern = functools.partial(_mlstm_kernel, chunk=ln)
    return pl.pallas_call(
        kern,
        grid=(b, t // ln),
        in_specs=[
            pl.BlockSpec((1, ln, hk), lambda i, c: (i, c, 0)),
            pl.BlockSpec((1, ln, hk), lambda i, c: (i, c, 1)),
            pl.BlockSpec((1, ln, hv), lambda i, c: (i, c, (2 * hk) // hv)),
            pl.BlockSpec((1, ln, hv), lambda i, c: (i, c, (2 * hk) // hv + 1)),
            pl.BlockSpec((1, ln, ng), lambda i, c: (i, c, 0)),
            pl.BlockSpec((1, ng, ln), lambda i, c: (i, 0, c)),
            pl.BlockSpec((1, hv), lambda i, c: (0, 0)),
        ],
        out_specs=pl.BlockSpec((1, ln, hv), lambda i, c: (i, c, 0)),
        out_shape=jax.ShapeDtypeStruct((b, t, hv), _BF),
        scratch_shapes=[
            pltpu.VMEM((M_HEADS, M_DQK, M_DV), _F32),
            pltpu.VMEM((M_HEADS, SUBLANES, M_DQK), _F32),
            pltpu.VMEM((M_HEADS, SUBLANES, LANES), _F32),
        ],
        compiler_params=_params("parallel", "arbitrary"),
        name=name,
    )(proj, proj, proj, proj, gates_col, gates_row, head_g.reshape(1, hv))


def _dsa_prep_kernel(q_ref, k_ref, qg_ref, kg_ref, qo_ref, ko_ref):
    qg = qg_ref[...]
    scale = A_DHEAD ** -0.5 * LOG2_E
    for h in range(A_HEADS):
        sl = slice(h * A_DHEAD, (h + 1) * A_DHEAD)
        qo_ref[:, sl] = (_rms_rows(q_ref[:, sl], qg) * scale).astype(_BF)
    ko_ref[...] = _rms_rows(k_ref[...], kg_ref[...]).astype(_BF)


def _dsa_prep(proj, q_g, k_g, *, tm, name):
    n = proj.shape[0]
    qd = A_HEADS * A_DHEAD
    return pl.pallas_call(
        _dsa_prep_kernel,
        grid=(n // tm,),
        in_specs=[
            pl.BlockSpec((tm, qd), lambda i: (i, 0)),
            pl.BlockSpec((tm, A_DHEAD), lambda i: (i, qd // A_DHEAD)),
            pl.BlockSpec((1, A_DHEAD), lambda i: (0, 0)),
            pl.BlockSpec((1, A_DHEAD), lambda i: (0, 0)),
        ],
        out_specs=[
            pl.BlockSpec((tm, qd), lambda i: (i, 0)),
            pl.BlockSpec((tm, A_DHEAD), lambda i: (i, 0)),
        ],
        out_shape=[jax.ShapeDtypeStruct((n, qd), _BF), jax.ShapeDtypeStruct((n, A_DHEAD), _BF)],
        compiler_params=_params("parallel"),
        name=name,
    )(proj, proj, q_g.reshape(1, A_DHEAD), k_g.reshape(1, A_DHEAD))


def _dsa_kernel(ki_ref, k_ref, vt_ref, qit_ref, wi_ref, qt_ref, o_ref,
                key_ref, cut_ref, acc_ref, m_ref, alpha_ref, p_ref, *, tk, topk):
    q0 = pl.program_id(1) * Q_BLOCK
    n_kt = (q0 + Q_BLOCK - 1) // tk + 1
    n_bits_idx = (key_ref.shape[0] - 1).bit_length()
    lane_q = q0 + lax.broadcasted_iota(jnp.int32, (tk, Q_BLOCK), 1)
    row = lax.broadcasted_iota(jnp.int32, (tk, Q_BLOCK), 0)

    qit = qit_ref[0, 0]
    wi = wi_ref[0, 0] * ((IDX_HEADS * IDX_DIM) ** -0.5)

    def score_tile(kt, carry):
        start = pl.multiple_of(kt * tk, tk)
        lg = _dot(ki_ref[0, pl.ds(start, tk), :], qit)
        w = jnp.maximum(lg, 0.0) * wi
        s = w[:, 0:Q_BLOCK]
        for h in range(1, IDX_HEADS):
            s = s + w[:, h * Q_BLOCK:(h + 1) * Q_BLOCK]
        bits = pltpu.bitcast(s, jnp.int32)
        key = bits ^ ((bits >> 31) & 0x7FFFFFFF)
        key = jnp.where(bits == INT_MIN, 0, key)
        key = jnp.where(start + row <= lane_q, key, INT_MIN)
        key_ref[pl.ds(start, tk), :] = key
        return carry

    lax.fori_loop(0, n_kt, score_tile, 0)

    def count(pred):
        def body(kt, acc):
            start = pl.multiple_of(kt * tk, tk)
            hit = pred(key_ref[pl.ds(start, tk), :], start + row).astype(jnp.int32)
            return acc + jnp.sum(hit.reshape(tk // SUBLANES, SUBLANES, Q_BLOCK), axis=0)
        acc = lax.fori_loop(0, n_kt, body, jnp.zeros((SUBLANES, Q_BLOCK), jnp.int32))
        return jnp.sum(acc, axis=0, keepdims=True)

    def thr_bit(b, t_u):
        cand_u = t_u | lax.shift_left(jnp.int32(1), 31 - b)
        cand = cand_u ^ INT_MIN
        cnt = count(lambda keys, idx: keys >= cand)
        return jnp.where(cnt >= topk, cand_u, t_u)

    thr = lax.fori_loop(0, 32, thr_bit, jnp.zeros((1, Q_BLOCK), jnp.int32)) ^ INT_MIN
    n_gt = count(lambda keys, idx: keys > thr)
    n_eq = count(lambda keys, idx: keys == thr)
    need = topk - n_gt
    cut_ref[...] = jnp.full((1, Q_BLOCK), 2 ** 30, jnp.int32)
    tied = jnp.logical_and(n_eq != need, thr != INT_MIN)

    @pl.when(jnp.max(tied.astype(jnp.int32)) > 0)
    def _():
        def cut_bit(b, p):
            cand = p | lax.shift_left(jnp.int32(1), n_bits_idx - 1 - b)
            cnt = count(lambda keys, idx: jnp.logical_and(keys == thr, idx < cand))
            return jnp.where(cnt < need, cand, p)
        cut_ref[...] = lax.fori_loop(0, n_bits_idx, cut_bit, jnp.zeros((1, Q_BLOCK), jnp.int32))

    cut = cut_ref[...]

    m_ref[...] = jnp.full(m_ref.shape, NEG_BIG, _F32)
    acc_ref[...] = jnp.zeros(acc_ref.shape, _F32)
    qt = qt_ref[0, 0]

    def att_tile(kt, carry):
        start = pl.multiple_of(kt * tk, tk)
        keys = key_ref[pl.ds(start, tk), :]
        sel = jnp.logical_or(keys > thr, jnp.logical_and(keys == thr, start + row <= cut))
        sel = jnp.logical_and(sel, keys != INT_MIN)
        mk = jnp.where(sel, 0.0, NEG_BIG)
        lg = _dot(k_ref[0, pl.ds(start, tk), :], qt)
        for h in range(A_HEADS):
            sl = slice(h * Q_BLOCK, (h + 1) * Q_BLOCK)
            x = lg[:, sl] + mk
            m_old = m_ref[:, sl]
            m_new = jnp.maximum(m_old, jnp.max(x, axis=0, keepdims=True))
            p_ref[:, sl] = jnp.exp2(x - m_new).astype(_BF)
            alpha_ref[:, sl] = jnp.exp2(m_old - m_new)
            m_ref[:, sl] = m_new
        pv = _dot(vt_ref[0, :, pl.ds(start, tk)], p_ref[...])
        acc_ref[...] = acc_ref[...] * alpha_ref[...] + pv
        return carry

    lax.fori_loop(0, n_kt, att_tile, 0)
    o_ref[0, 0] = (acc_ref[0:A_DHEAD, :] / acc_ref[A_DHEAD:A_DHEAD + 1, :]).astype(o_ref.dtype)


def _dsa_attend(ki, k, vt, qit, wi, qt, *, name):
    b, t, _ = ki.shape
    nb = t // Q_BLOCK
    hq = A_HEADS * Q_BLOCK
    topk = min(TOPK_MAX, t // 4)
    kern = functools.partial(_dsa_kernel, tk=KEY_TILE, topk=topk)
    return pl.pallas_call(
        kern,
        grid=(b, nb),
        in_specs=[
            pl.BlockSpec((1, t, IDX_DIM), lambda i, j: (i, 0, 0)),
            pl.BlockSpec((1, t, A_DHEAD), lambda i, j: (i, 0, 0)),
            pl.BlockSpec((1, vt.shape[1], t), lambda i, j: (i, 0, 0)),
            pl.BlockSpec((1, 1, IDX_DIM, hq), lambda i, j: (i, j, 0, 0)),
            pl.BlockSpec((1, 1, 1, hq), lambda i, j: (i, j, 0, 0)),
            pl.BlockSpec((1, 1, A_DHEAD, hq), lambda i, j: (i, j, 0, 0)),
        ],
        out_specs=pl.BlockSpec((1, 1, A_DHEAD, hq), lambda i, j: (i, j, 0, 0)),
        out_shape=jax.ShapeDtypeStruct((b, nb, A_DHEAD, hq), _BF),
        scratch_shapes=[
            pltpu.VMEM((t, Q_BLOCK), jnp.int32),
            pltpu.VMEM((1, Q_BLOCK), jnp.int32),
            pltpu.VMEM((vt.shape[1], hq), _F32),
            pltpu.VMEM((1, hq), _F32),
            pltpu.VMEM((1, hq), _F32),
            pltpu.VMEM((KEY_TILE, hq), _BF),
        ],
        compiler_params=_params("parallel", "arbitrary"),
        name=name,
    )(ki, k, vt, qit, wi, qt)


def _mlstm_layer(x2, norm_g, w_in, gate_b, head_g, w_out, *, batch, seq):
    n, d = x2.shape
    hk = M_HEADS * M_DQK
    hv = M_HEADS * M_DV
    main = 2 * hk + 2 * hv
    proj = _norm_matmul(x2, norm_g, w_in[:, :main].astype(_BF), tm=512, tn=1024, name="mlstm_in")
    gates = _gates(x2, norm_g, w_in[:, main:], gate_b, tm=512, name="mlstm_gates")
    gates = gates.reshape(batch, seq, 2 * M_HEADS)
    h = _mlstm_core(proj.reshape(batch, seq, main), gates, gates.transpose(0, 2, 1), head_g, name="mlstm_core")
    return _matmul_resid(h.reshape(n, hv), w_out.astype(_BF), x2, tm=512, tn=d, name="mlstm_out")


def _dsa_layer(x2, norm_g, w_in, q_g, k_g, w_out, *, batch, seq):
    n, d = x2.shape
    qd = A_HEADS * A_DHEAD
    qid = IDX_HEADS * IDX_DIM
    o_k, o_v, o_qi = qd, qd + A_DHEAD, qd + 2 * A_DHEAD
    o_ki, o_wi, o_end = o_qi + qid, o_qi + qid + IDX_DIM, o_qi + qid + IDX_DIM + IDX_HEADS
    tn = 512
    padded = -(-o_end // tn) * tn
    w_pad = jnp.zeros((d, padded), _BF).at[:, :o_end].set(w_in.astype(_BF))
    proj = _norm_matmul(x2, norm_g, w_pad, tm=512, tn=tn, name="dsa_in")
    qn, kn = _dsa_prep(proj, q_g, k_g, tm=512, name="dsa_qknorm")
    nb = seq // Q_BLOCK
    qt = qn.reshape(batch, nb, Q_BLOCK, A_HEADS, A_DHEAD).transpose(0, 1, 4, 3, 2).reshape(batch, nb, A_DHEAD, A_HEADS * Q_BLOCK)
    qit = proj[:, o_qi:o_ki].astype(_BF).reshape(batch, nb, Q_BLOCK, IDX_HEADS, IDX_DIM)
    qit = qit.transpose(0, 1, 4, 3, 2).reshape(batch, nb, IDX_DIM, IDX_HEADS * Q_BLOCK)
    wi = proj[:, o_wi:o_end].reshape(batch, nb, Q_BLOCK, IDX_HEADS).transpose(0, 1, 3, 2).reshape(batch, nb, 1, IDX_HEADS * Q_BLOCK)
    ki = proj[:, o_ki:o_wi].astype(_BF).reshape(batch, seq, IDX_DIM)
    vt = proj[:, o_v:o_qi].astype(_BF).reshape(batch, seq, A_DHEAD).transpose(0, 2, 1)
    extra = (lax.broadcasted_iota(jnp.int32, (batch, SUBLANES, seq), 1) == 0).astype(_BF)
    vt = jnp.concatenate([vt, extra], axis=1)
    out_t = _dsa_attend(ki, kn.reshape(batch, seq, A_DHEAD), vt, qit, wi, qt, name="dsa_attend")
    out = out_t.reshape(batch, nb, A_DHEAD, A_HEADS, Q_BLOCK).transpose(0, 1, 4, 3, 2).reshape(n, qd)
    return _matmul_resid(out, w_out.astype(_BF), x2, tm=512, tn=d, name="dsa_out")


def _conv_layer(x2, norm_g, w_in, dw_w, dw_b, ln_g, ln_b, w_out, *, seq):
    u = _norm_glu(x2, norm_g, w_in.astype(_BF), tm=512, tn=512, name="conv_in")
    v = _conv_mid(u, dw_w, dw_b, ln_g, ln_b, seq=seq, tm=256, name="conv_mid")
    return _matmul_resid(v, w_out.astype(_BF), x2, tm=512, tn=x2.shape[1], name="conv_out")


def _ffn_layer(x2, norm_g, w_up, dw, w_down, *, seq):
    return _ffn(x2, norm_g, w_up.astype(_BF), dw, w_down.astype(_BF), seq=seq, tm=512, tn=512, name="ffn")


def kernel(x, mix_norm_g, ffn_norm_g, mlstm_w_in, mlstm_gate_b, mlstm_head_g, mlstm_w_out,
           dsa_w_in, dsa_q_g, dsa_k_g, dsa_w_out, conv_w_in, conv_dw_w, conv_dw_b,
           conv_ln_g, conv_ln_b, conv_w_out, ffn_w_up, ffn_dw_w, ffn_w_down):
    batch, seq, d = x.shape
    depth = mix_norm_g.shape[0]
    x2 = x.reshape(batch * seq, d)
    for i in range(depth):
        kind = i % N_MIXERS
        j = i // N_MIXERS
        if kind == 0:
            x2 = _mlstm_layer(x2, mix_norm_g[i], mlstm_w_in[j], mlstm_gate_b[j], mlstm_head_g[j], mlstm_w_out[j],
                              batch=batch, seq=seq)
        elif kind == 1:
            x2 = _dsa_layer(x2, mix_norm_g[i], dsa_w_in[j], dsa_q_g[j], dsa_k_g[j], dsa_w_out[j], batch=batch, seq=seq)
        else:
            x2 = _conv_layer(x2, mix_norm_g[i], conv_w_in[j], conv_dw_w[j], conv_dw_b[j], conv_ln_g[j], conv_ln_b[j],
                             conv_w_out[j], seq=seq)
        x2 = _ffn_layer(x2, ffn_norm_g[i], ffn_w_up[i], ffn_dw_w[i], ffn_w_down[i], seq=seq)
    return x2.reshape(batch, seq, d)
```

```python
import functools

import jax
import jax.numpy as jnp
from jax import lax
from jax.experimental import pallas as pl
from jax.experimental.pallas import tpu as pltpu

N_MIXERS = 3
EPS = 1e-6
M_HEADS = 8
M_DQK = 128
M_DV = 256
A_HEADS = 16
A_DHEAD = 128
IDX_HEADS = 16
IDX_DIM = 64
TOPK_MAX = 256
CONV_WIDTH = 31
FFN_CONV_WIDTH = 3

LANES = 128
SUBLANES = 8
FFN_HALO = 16
VMEM_LIMIT = 52 * 1024 * 1024
MLSTM_CHUNK = 256
Q_BLOCK = 128
KEY_TILE = 256
CONV_HALO = 32
LOG2_E = 1.4426950408889634
INT_MIN = -(2 ** 31)
NEG_BIG = -1e30

_BF = jnp.bfloat16
_F32 = jnp.float32


def _params(*sem):
    return pltpu.CompilerParams(dimension_semantics=sem, vmem_limit_bytes=VMEM_LIMIT)


def _rms_rows(x, g):
    ms = jnp.mean(x * x, axis=-1, keepdims=True)
    return x * lax.rsqrt(ms + EPS) * g


def _dot(a, b):
    return jnp.dot(a, b, preferred_element_type=_F32)


def _split3(x):
    hi = x.astype(_BF)
    r1 = x - hi.astype(_F32)
    mid = r1.astype(_BF)
    lo = (r1 - mid.astype(_F32)).astype(_BF)
    return hi, mid, lo


def _norm_matmul_kernel(x_ref, g_ref, w_ref, o_ref, xn_ref):
    @pl.when(pl.program_id(1) == 0)
    def _():
        xn_ref[...] = _rms_rows(x_ref[...], g_ref[...]).astype(_BF)

    o_ref[...] = _dot(xn_ref[...], w_ref[...]).astype(o_ref.dtype)


def _norm_matmul(x, g, w, *, tm, tn, name):
    n, d = x.shape
    m = w.shape[1]
    return pl.pallas_call(
        _norm_matmul_kernel,
        grid=(n // tm, m // tn),
        in_specs=[
            pl.BlockSpec((tm, d), lambda i, j: (i, 0)),
            pl.BlockSpec((1, d), lambda i, j: (0, 0)),
            pl.BlockSpec((d, tn), lambda i, j: (0, j)),
        ],
        out_specs=pl.BlockSpec((tm, tn), lambda i, j: (i, j)),
        out_shape=jax.ShapeDtypeStruct((n, m), _F32),
        scratch_shapes=[pltpu.VMEM((tm, d), _BF)],
        compiler_params=_params("parallel", "arbitrary"),
        name=name,
    )(x, g.reshape(1, d), w)


def _norm_glu_kernel(x_ref, g_ref, wa_ref, wg_ref, o_ref, xn_ref):
    @pl.when(pl.program_id(1) == 0)
    def _():
        xn_ref[...] = _rms_rows(x_ref[...], g_ref[...]).astype(_BF)

    xn = xn_ref[...]
    a = _dot(xn, wa_ref[...])
    gate = _dot(xn, wg_ref[...])
    o_ref[...] = a * jax.nn.sigmoid(gate)


def _norm_glu(x, g, w, *, tm, tn, name):
    n, d = x.shape
    half = w.shape[1] // 2
    nj = half // tn
    return pl.pallas_call(
        _norm_glu_kernel,
        grid=(n // tm, nj),
        in_specs=[
            pl.BlockSpec((tm, d), lambda i, j: (i, 0)),
            pl.BlockSpec((1, d), lambda i, j: (0, 0)),
            pl.BlockSpec((d, tn), lambda i, j: (0, j)),
            pl.BlockSpec((d, tn), lambda i, j: (0, j + nj)),
        ],
        out_specs=pl.BlockSpec((tm, tn), lambda i, j: (i, j)),
        out_shape=jax.ShapeDtypeStruct((n, half), _F32),
        scratch_shapes=[pltpu.VMEM((tm, d), _BF)],
        compiler_params=_params("parallel", "arbitrary"),
        name=name,
    )(x, g.reshape(1, d), w, w)


def _matmul_resid_kernel(a_ref, w_ref, r_ref, o_ref):
    o_ref[...] = r_ref[...] + _dot(a_ref[...], w_ref[...])


def _matmul_resid(a, w, resid, *, tm, tn, name):
    n, k = a.shape
    m = w.shape[1]
    return pl.pallas_call(
        _matmul_resid_kernel,
        grid=(n // tm, m // tn),
        in_specs=[
            pl.BlockSpec((tm, k), lambda i, j: (i, 0)),
            pl.BlockSpec((k, tn), lambda i, j: (0, j)),
            pl.BlockSpec((tm, tn), lambda i, j: (i, j)),
        ],
        out_specs=pl.BlockSpec((tm, tn), lambda i, j: (i, j)),
        out_shape=jax.ShapeDtypeStruct((n, m), _F32),
        compiler_params=_params("parallel", "arbitrary"),
        name=name,
    )(a, w, resid)


def _ffn_kernel(x_ref, xh_ref, g_ref, wa_ref, wb_ref, dwa_ref, dwb_ref, wd_ref, o_ref,
                xn_ref, ha0_ref, hb0_ref, ha1_ref, hb1_ref, act_ref, *, tm, nj, total, tiles_per_seq, act_rows):
    s = pl.program_id(0)

    @pl.when(s == 0)
    def _():
        ha1_ref[...] = jnp.zeros_like(ha1_ref)
        hb1_ref[...] = jnp.zeros_like(hb1_ref)
        o_ref[...] = jnp.zeros_like(o_ref)

    @pl.when(jnp.logical_and(s % nj == 0, s < total))
    def _():
        g = g_ref[...]
        halo = _rms_rows(xh_ref[...], g)
        halo = jnp.where((s // nj) % tiles_per_seq == 0, 0.0, halo)
        xn_ref[0:FFN_HALO, :] = halo.astype(_BF)
        xn_ref[FFN_HALO:, :] = _rms_rows(x_ref[...], g).astype(_BF)

    @pl.when((s - 1) % nj == 0)
    def _():
        o_ref[...] = x_ref[...]

    def conv(h_ref, dw, r0, rows):
        base = r0 + FFN_HALO - (FFN_CONV_WIDTH - 1)
        out = dw[0:1, :] * h_ref[base:base + rows, :]
        for k in range(1, FFN_CONV_WIDTH):
            out = out + dw[k:k + 1, :] * h_ref[base + k:base + k + rows, :]
        return out

    def step(ha_prev, hb_prev, ha_next, hb_next):
        ha_next[...] = _dot(xn_ref[...], wa_ref[...])
        hb_next[...] = _dot(xn_ref[...], wb_ref[...])
        dwa = dwa_ref[...]
        dwb = dwb_ref[...]
        for r0 in range(0, tm, act_rows):
            a = conv(ha_prev, dwa, r0, act_rows)
            b = conv(hb_prev, dwb, r0, act_rows)
            act_ref[r0:r0 + act_rows, :] = (a * jax.nn.sigmoid(a) * b).astype(_BF)
        o_ref[...] += _dot(act_ref[...], wd_ref[...])

    @pl.when(s % 2 == 0)
    def _():
        step(ha1_ref, hb1_ref, ha0_ref, hb0_ref)

    @pl.when(s % 2 == 1)
    def _():
        step(ha0_ref, hb0_ref, ha1_ref, hb1_ref)


def _ffn(x, g, w_up, dw, w_down, *, seq, tm, tn, name):
    n, d = x.shape
    f = w_up.shape[1] // 2
    nj = f // tn
    total = (n // tm) * nj
    hblk = tm // FFN_HALO
    kern = functools.partial(_ffn_kernel, tm=tm, nj=nj, total=total, tiles_per_seq=seq // tm, act_rows=64)

    def cur(s):
        c = jnp.minimum(s, total - 1)
        return c // nj, c % nj

    def prev(s):
        p = jnp.maximum(s - 1, 0)
        return p // nj, p % nj

    h_scratch = pltpu.VMEM((tm + FFN_HALO, tn), _F32)
    return pl.pallas_call(
        kern,
        grid=(total + 1,),
        in_specs=[
            pl.BlockSpec((tm, d), lambda s: (cur(s)[0], 0)),
            pl.BlockSpec((FFN_HALO, d), lambda s: (jnp.maximum(cur(s)[0] * hblk - 1, 0), 0)),
            pl.BlockSpec((1, d), lambda s: (0, 0)),
            pl.BlockSpec((d, tn), lambda s: (0, cur(s)[1])),
            pl.BlockSpec((d, tn), lambda s: (0, cur(s)[1] + nj)),
            pl.BlockSpec((FFN_CONV_WIDTH, tn), lambda s: (0, prev(s)[1])),
            pl.BlockSpec((FFN_CONV_WIDTH, tn), lambda s: (0, prev(s)[1] + nj)),
            pl.BlockSpec((tn, d), lambda s: (prev(s)[1], 0)),
        ],
        out_specs=pl.BlockSpec((tm, d), lambda s: (prev(s)[0], 0)),
        out_shape=jax.ShapeDtypeStruct((n, d), _F32),
        scratch_shapes=[pltpu.VMEM((tm + FFN_HALO, d), _BF), h_scratch, h_scratch, h_scratch, h_scratch,
                        pltpu.VMEM((tm, tn), _BF)],
        compiler_params=_params("arbitrary"),
        name=name,
    )(x, x, g.reshape(1, d), w_up, w_up, dw, dw, w_down)


def _conv_mid_kernel(u_ref, uh_ref, dw_ref, db_ref, lg_ref, lb_ref, o_ref, ext_ref, c_ref, sh_ref, *, tm, tiles_per_seq, rb, cb):
    i = pl.program_id(0)
    d = u_ref.shape[1]
    ext_ref[0:CONV_HALO, :] = jnp.where(i % tiles_per_seq == 0, 0.0, uh_ref[...])
    ext_ref[CONV_HALO:, :] = u_ref[...]
    base = CONV_HALO - (CONV_WIDTH - 1)

    def col_block(c, carry):
        cs = pl.multiple_of(c * cb, cb)
        for r in range(tm // rb):
            acc = jnp.zeros((rb, cb), _F32) + db_ref[:, pl.ds(cs, cb)]
            for phase in range(SUBLANES):
                taps = [k for k in range(CONV_WIDTH) if (base + k) % SUBLANES == phase]
                span = rb + max(base + k - phase for k in taps)
                sh_ref[0:span, :] = ext_ref[r * rb + phase:r * rb + phase + span, pl.ds(cs, cb)]
                for k in taps:
                    off = base + k - phase
                    acc = acc + sh_ref[off:off + rb, :] * dw_ref[k:k + 1, pl.ds(cs, cb)]
            c_ref[r * rb:(r + 1) * rb, pl.ds(cs, cb)] = acc
        return carry

    lax.fori_loop(0, d // cb, col_block, 0)
    c = c_ref[...]
    mu = jnp.mean(c, axis=-1, keepdims=True)
    xc = c - mu
    var = jnp.mean(xc * xc, axis=-1, keepdims=True)
    y = xc * lax.rsqrt(var + EPS) * lg_ref[...] + lb_ref[...]
    o_ref[...] = (y * jax.nn.sigmoid(y)).astype(o_ref.dtype)


def _conv_mid(u, dw, db, lg, lb, *, seq, tm, name):
    n, d = u.shape
    hblk = tm // CONV_HALO
    rb, cb = 64, 256
    kern = functools.partial(_conv_mid_kernel, tm=tm, tiles_per_seq=seq // tm, rb=rb, cb=cb)
    return pl.pallas_call(
        kern,
        grid=(n // tm,),
        in_specs=[
            pl.BlockSpec((tm, d), lambda i: (i, 0)),
            pl.BlockSpec((CONV_HALO, d), lambda i: (jnp.maximum(i * hblk - 1, 0), 0)),
            pl.BlockSpec((CONV_WIDTH, d), lambda i: (0, 0)),
            pl.BlockSpec((1, d), lambda i: (0, 0)),
            pl.BlockSpec((1, d), lambda i: (0, 0)),
            pl.BlockSpec((1, d), lambda i: (0, 0)),
        ],
        out_specs=pl.BlockSpec((tm, d), lambda i: (i, 0)),
        out_shape=jax.ShapeDtypeStruct((n, d), _BF),
        scratch_shapes=[pltpu.VMEM((tm + CONV_HALO, d), _F32), pltpu.VMEM((tm, d), _F32),
                        pltpu.VMEM((rb + CONV_HALO, cb), _F32)],
        compiler_params=_params("parallel"),
        name=name,
    )(u, u, dw, db.reshape(1, d), lg.reshape(1, d), lb.reshape(1, d))


def _gates_kernel(x_ref, g_ref, whi_ref, wlo_ref, b_ref, o_ref):
    xn = _rms_rows(x_ref[...], g_ref[...])
    xhi = xn.astype(_BF)
    xlo = (xn - xhi.astype(_F32)).astype(_BF)
    whi = whi_ref[...]
    z = _dot(xhi, whi) + _dot(xlo, whi) + _dot(xhi, wlo_ref[...]) + b_ref[...]
    log_f = jnp.minimum(z, 0.0) - jnp.log(1.0 + jnp.exp(-jnp.abs(z)))
    col = lax.broadcasted_iota(jnp.int32, z.shape, 1)
    o_ref[...] = jnp.where(col < M_HEADS, z, log_f)


def _gates(x, g, w_gate, bias, *, tm, name):
    n, d = x.shape
    ng = w_gate.shape[1]
    wp = jnp.zeros((d, LANES), _F32).at[:, :ng].set(w_gate)
    whi = wp.astype(_BF)
    wlo = (wp - whi.astype(_F32)).astype(_BF)
    bp = jnp.zeros((1, LANES), _F32).at[0, :ng].set(bias)
    out = pl.pallas_call(
        _gates_kernel,
        grid=(n // tm,),
        in_specs=[
            pl.BlockSpec((tm, d), lambda i: (i, 0)),
            pl.BlockSpec((1, d), lambda i: (0, 0)),
            pl.BlockSpec((d, LANES), lambda i: (0, 0)),
            pl.BlockSpec((d, LANES), lambda i: (0, 0)),
            pl.BlockSpec((1, LANES), lambda i: (0, 0)),
        ],
        out_specs=pl.BlockSpec((tm, LANES), lambda i: (i, 0)),
        out_shape=jax.ShapeDtypeStruct((n, LANES), _F32),
        compiler_params=_params("parallel"),
        name=name,
    )(x, g.reshape(1, d), whi, wlo, bp)
    return out[:, :ng]


def _mlstm_kernel(q_ref, k_ref, v_ref, o_ref, gc_ref, gr_ref, hg_ref, out_ref, c_ref, n_ref, m_ref, *, chunk):
    ln = chunk
    scale = M_DQK ** -0.5

    @pl.when(pl.program_id(1) == 0)
    def _():
        c_ref[...] = jnp.zeros_like(c_ref)
        n_ref[...] = jnp.zeros_like(n_ref)
        m_ref[...] = jnp.zeros_like(m_ref)

    gc = gc_ref[0]
    gr = gr_ref[0]
    rows = lax.broadcasted_iota(jnp.int32, (ln, ln), 0)
    cols = lax.broadcasted_iota(jnp.int32, (ln, ln), 1)
    causal = rows >= cols
    tril = causal.astype(_BF)
    triu = (rows <= cols).astype(_BF)
    cum_col = sum(_dot(tril, p) for p in _split3(gc))
    cum_row = sum(_dot(p, triu) for p in _split3(gr))

    for h in range(M_HEADS):
        f = M_HEADS + h
        bcol = cum_col[:, f:f + 1]
        brow = cum_row[f:f + 1, :]
        icol = gc[:, h:h + 1]
        irow = gr[h:h + 1, :]
        m_prev = m_ref[h, 0:1, 0:1]
        nrow = n_ref[h, 0:1, :]
        qh = q_ref[0, :, h * M_DQK:(h + 1) * M_DQK]
        kh = k_ref[0, :, h * M_DQK:(h + 1) * M_DQK]
        vb = v_ref[0, :, h * M_DV:(h + 1) * M_DV].astype(_BF)
        qb = qh.astype(_BF)

        dmat = jnp.where(causal, bcol - brow + irow, -jnp.inf)
        inter = bcol + m_prev
        m_t = jnp.maximum(inter, jnp.max(dmat, axis=1, keepdims=True))
        w_intra = jnp.exp(dmat - m_t)
        w_inter = jnp.exp(inter - m_t)
        s = lax.dot_general(qb, kh.astype(_BF), (((1,), (1,)), ((), ())), preferred_element_type=_F32)
        s = s * (w_intra * scale)
        qc = _dot(qb, c_ref[h].astype(_BF)) * scale
        num = w_inter * qc + _dot(s.astype(_BF), vb)
        qn = jnp.sum(qh * nrow, axis=1, keepdims=True) * scale
        den = w_inter * qn + jnp.sum(s, axis=1, keepdims=True)
        hh = num / jnp.maximum(jnp.abs(den), jnp.exp(-m_t))

        bl = bcol[ln - 1:ln, :]
        g_col = bl - bcol + icol
        g_row = bl - brow + irow
        m_new = jnp.maximum(bl + m_prev, jnp.max(g_row, axis=1, keepdims=True))
        a_prev = jnp.exp(bl + m_prev - m_new)
        kw = kh * jnp.exp(g_col - m_new)
        upd = lax.dot_general(kw.astype(_BF), vb, (((0,), (0,)), ((), ())), preferred_element_type=_F32)
        c_ref[h] = a_prev * c_ref[h] + upd
        n_new = a_prev * nrow + jnp.sum(kw, axis=0, keepdims=True)
        n_ref[h] = jnp.broadcast_to(n_new, (SUBLANES, M_DQK))
        m_ref[h] = jnp.broadcast_to(m_new, (SUBLANES, LANES))

        hn = hh * lax.rsqrt(jnp.mean(hh * hh, axis=1, keepdims=True) + EPS) * hg_ref[:, h * M_DV:(h + 1) * M_DV]
        og = o_ref[0, :, h * M_DV:(h + 1) * M_DV]
        out_ref[0, :, h * M_DV:(h + 1) * M_DV] = (hn * jax.nn.sigmoid(og)).astype(out_ref.dtype)


def _mlstm_core(proj, gates_col, gates_row, head_g, *, name):
    b, t, _ = proj.shape
    hk = M_HEADS * M_DQK
    hv = M_HEADS * M_DV
    ln = MLSTM_CHUNK
    ng = 2 * M_HEADS
    kern = functools.partial(_mlstm_kernel, chunk=ln)
    return pl.pallas_call(
        kern,
        grid=(b, t // ln),
        in_specs=[
            pl.BlockSpec((1, ln, hk), lambda i, c: (i, c, 0)),
            pl.BlockSpec((1, ln, hk), lambda i, c: (i, c, 1)),
            pl.BlockSpec((1, ln, hv), lambda i, c: (i, c, (2 * hk) // hv)),
            pl.BlockSpec((1, ln, hv), lambda i, c: (i, c, (2 * hk) // hv + 1)),
            pl.BlockSpec((1, ln, ng), lambda i, c: (i, c, 0)),
            pl.BlockSpec((1, ng, ln), lambda i, c: (i, 0, c)),
            pl.BlockSpec((1, hv), lambda i, c: (0, 0)),
        ],
        out_specs=pl.BlockSpec((1, ln, hv), lambda i, c: (i, c, 0)),
        out_shape=jax.ShapeDtypeStruct((b, t, hv), _BF),
        scratch_shapes=[
            pltpu.VMEM((M_HEADS, M_DQK, M_DV), _F32),
            pltpu.VMEM((M_HEADS, SUBLANES, M_DQK), _F32),
            pltpu.VMEM((M_HEADS, SUBLANES, LANES), _F32),
        ],
        compiler_params=_params("parallel", "arbitrary"),
        name=name,
    )(proj, proj, proj, proj, gates_col, gates_row, head_g.reshape(1, hv))


def _dsa_prep_kernel(p_ref, qg_ref, kg_ref, qt_ref, qit_ref, k_ref, ki_ref, vt_ref, *, offs):
    o_k, o_v, o_qi, o_ki = offs
    qg = qg_ref[...]
    scale = A_DHEAD ** -0.5 * LOG2_E
    for h in range(A_HEADS):
        sl = slice(h * A_DHEAD, (h + 1) * A_DHEAD)
        qh = _rms_rows(p_ref[:, sl], qg) * scale
        qt_ref[0, 0, :, h * Q_BLOCK:(h + 1) * Q_BLOCK] = qh.T.astype(_BF)
    pair = LANES // IDX_DIM
    for g in range(IDX_HEADS // pair):
        slab = p_ref[:, o_qi + g * LANES:o_qi + (g + 1) * LANES].T
        for e in range(pair):
            h = g * pair + e
            qit_ref[0, 0, :, h * Q_BLOCK:(h + 1) * Q_BLOCK] = slab[e * IDX_DIM:(e + 1) * IDX_DIM, :].astype(_BF)
    k_ref[...] = _rms_rows(p_ref[:, o_k:o_v], kg_ref[...]).astype(_BF)
    ki_ref[...] = p_ref[:, o_ki:o_ki + IDX_DIM].astype(_BF)
    vt_ref[0, 0:A_DHEAD, :] = p_ref[:, o_v:o_qi].T.astype(_BF)
    ones_row = lax.broadcasted_iota(jnp.int32, (SUBLANES, Q_BLOCK), 0) == 0
    vt_ref[0, A_DHEAD:, :] = ones_row.astype(_F32).astype(_BF)


def _dsa_prep(proj, q_g, k_g, *, batch, seq, offs, name):
    n, width = proj.shape
    nb = seq // Q_BLOCK
    hq = A_HEADS * Q_BLOCK
    kern = functools.partial(_dsa_prep_kernel, offs=offs)
    return pl.pallas_call(
        kern,
        grid=(batch, nb),
        in_specs=[
            pl.BlockSpec((Q_BLOCK, width), lambda i, j: (i * nb + j, 0)),
            pl.BlockSpec((1, A_DHEAD), lambda i, j: (0, 0)),
            pl.BlockSpec((1, A_DHEAD), lambda i, j: (0, 0)),
        ],
        out_specs=[
            pl.BlockSpec((1, 1, A_DHEAD, hq), lambda i, j: (i, j, 0, 0)),
            pl.BlockSpec((1, 1, IDX_DIM, hq), lambda i, j: (i, j, 0, 0)),
            pl.BlockSpec((Q_BLOCK, A_DHEAD), lambda i, j: (i * nb + j, 0)),
            pl.BlockSpec((Q_BLOCK, IDX_DIM), lambda i, j: (i * nb + j, 0)),
            pl.BlockSpec((1, A_DHEAD + SUBLANES, Q_BLOCK), lambda i, j: (i, 0, j)),
        ],
        out_shape=[
            jax.ShapeDtypeStruct((batch, nb, A_DHEAD, hq), _BF),
            jax.ShapeDtypeStruct((batch, nb, IDX_DIM, hq), _BF),
            jax.ShapeDtypeStruct((n, A_DHEAD), _BF),
            jax.ShapeDtypeStruct((n, IDX_DIM), _BF),
            jax.ShapeDtypeStruct((batch, A_DHEAD + SUBLANES, seq), _BF),
        ],
        compiler_params=_params("parallel", "parallel"),
        name=name,
    )(proj, q_g.reshape(1, A_DHEAD), k_g.reshape(1, A_DHEAD))


def _dsa_kernel(ki_ref, k_ref, vt_ref, qit_ref, wi_ref, qt_ref, o_ref,
                key_ref, cut_ref, acc_ref, m_ref, alpha_ref, p_ref, *, tk, topk):
    q0 = pl.program_id(1) * Q_BLOCK
    n_kt = (q0 + Q_BLOCK - 1) // tk + 1
    n_bits_idx = (key_ref.shape[0] - 1).bit_length()
    lane_q = q0 + lax.broadcasted_iota(jnp.int32, (tk, Q_BLOCK), 1)
    row = lax.broadcasted_iota(jnp.int32, (tk, Q_BLOCK), 0)

    qit = qit_ref[0, 0]
    wi = wi_ref[0, 0] * ((IDX_HEADS * IDX_DIM) ** -0.5)

    def score_tile(kt, carry):
        start = pl.multiple_of(kt * tk, tk)
        lg = _dot(ki_ref[0, pl.ds(start, tk), :], qit)
        w = jnp.maximum(lg, 0.0) * wi
        s = w[:, 0:Q_BLOCK]
        for h in range(1, IDX_HEADS):
            s = s + w[:, h * Q_BLOCK:(h + 1) * Q_BLOCK]
        bits = pltpu.bitcast(s, jnp.int32)
        key = bits ^ ((bits >> 31) & 0x7FFFFFFF)
        key = jnp.where(bits == INT_MIN, 0, key)
        key = jnp.where(start + row <= lane_q, key, INT_MIN)
        key_ref[pl.ds(start, tk), :] = key
        return carry

    lax.fori_loop(0, n_kt, score_tile, 0)

    ck = 2 * tk
    n_ct = (n_kt + 1) // 2

    @pl.when(n_kt % 2 == 1)
    def _():
        key_ref[pl.ds(pl.multiple_of(n_kt * tk, tk), tk), :] = jnp.full((tk, Q_BLOCK), INT_MIN, jnp.int32)

    crow = lax.broadcasted_iota(jnp.int32, (ck, Q_BLOCK), 0)

    def count(pred):
        def body(ct, acc):
            start = pl.multiple_of(ct * ck, ck)
            hit = pred(key_ref[pl.ds(start, ck), :], start + crow).astype(jnp.int32)
            return acc + jnp.sum(hit.reshape(ck // SUBLANES, SUBLANES, Q_BLOCK), axis=0)
        acc = lax.fori_loop(0, n_ct, body, jnp.zeros((SUBLANES, Q_BLOCK), jnp.int32))
        return jnp.sum(acc, axis=0, keepdims=True)

    def thr_bit(b, t_u):
        cand_u = t_u | lax.shift_left(jnp.int32(1), 31 - b)
        cand = cand_u ^ INT_MIN
        cnt = count(lambda keys, idx: keys >= cand)
        return jnp.where(cnt >= topk, cand_u, t_u)

    thr = lax.fori_loop(0, 32, thr_bit, jnp.zeros((1, Q_BLOCK), jnp.int32)) ^ INT_MIN
    n_gt = count(lambda keys, idx: keys > thr)
    n_eq = count(lambda keys, idx: keys == thr)
    need = topk - n_gt
    cut_ref[...] = jnp.full((1, Q_BLOCK), 2 ** 30, jnp.int32)
    tied = jnp.logical_and(n_eq != need, thr != INT_MIN)

    @pl.when(jnp.max(tied.astype(jnp.int32)) > 0)
    def _():
        def cut_bit(b, p):
            cand = p | lax.shift_left(jnp.int32(1), n_bits_idx - 1 - b)
            cnt = count(lambda keys, idx: jnp.logical_and(keys == thr, idx < cand))
            return jnp.where(cnt < need, cand, p)
        cut_ref[...] = lax.fori_loop(0, n_bits_idx, cut_bit, jnp.zeros((1, Q_BLOCK), jnp.int32))

    cut = cut_ref[...]

    m_ref[...] = jnp.full(m_ref.shape, NEG_BIG, _F32)
    acc_ref[...] = jnp.zeros(acc_ref.shape, _F32)
    qt = qt_ref[0, 0]

    def att_tile(kt, carry):
        start = pl.multiple_of(kt * tk, tk)
        keys = key_ref[pl.ds(start, tk), :]
        sel = jnp.logical_or(keys > thr, jnp.logical_and(keys == thr, start + row <= cut))
        sel = jnp.logical_and(sel, keys != INT_MIN)
        mk = jnp.where(sel, 0.0, NEG_BIG)
        lg = _dot(k_ref[0, pl.ds(start, tk), :], qt)
        for h in range(A_HEADS):
            sl = slice(h * Q_BLOCK, (h + 1) * Q_BLOCK)
            x = lg[:, sl] + mk
            m_old = m_ref[:, sl]
            m_new = jnp.maximum(m_old, jnp.max(x, axis=0, keepdims=True))
            p_ref[:, sl] = jnp.exp2(x - m_new).astype(_BF)
            alpha_ref[:, sl] = jnp.exp2(m_old - m_new)
            m_ref[:, sl] = m_new
        pv = _dot(vt_ref[0, :, pl.ds(start, tk)], p_ref[...])
        acc_ref[...] = acc_ref[...] * alpha_ref[...] + pv
        return carry

    lax.fori_loop(0, n_kt, att_tile, 0)
    for h in range(A_HEADS):
        sl = slice(h * Q_BLOCK, (h + 1) * Q_BLOCK)
        out_h = acc_ref[0:A_DHEAD, sl] / acc_ref[A_DHEAD:A_DHEAD + 1, sl]
        o_ref[0, :, h * A_DHEAD:(h + 1) * A_DHEAD] = out_h.T.astype(o_ref.dtype)


def _dsa_attend(ki, k, vt, qit, wi, qt, *, name):
    b, t, _ = ki.shape
    nb = t // Q_BLOCK
    hq = A_HEADS * Q_BLOCK
    topk = min(TOPK_MAX, t // 4)
    kern = functools.partial(_dsa_kernel, tk=KEY_TILE, topk=topk)
    return pl.pallas_call(
        kern,
        grid=(b, nb),
        in_specs=[
            pl.BlockSpec((1, t, IDX_DIM), lambda i, j: (i, 0, 0)),
            pl.BlockSpec((1, t, A_DHEAD), lambda i, j: (i, 0, 0)),
            pl.BlockSpec((1, vt.shape[1], t), lambda i, j: (i, 0, 0)),
            pl.BlockSpec((1, 1, IDX_DIM, hq), lambda i, j: (i, j, 0, 0)),
            pl.BlockSpec((1, 1, 1, hq), lambda i, j: (i, j, 0, 0)),
            pl.BlockSpec((1, 1, A_DHEAD, hq), lambda i, j: (i, j, 0, 0)),
        ],
        out_specs=pl.BlockSpec((1, Q_BLOCK, A_HEADS * A_DHEAD), lambda i, j: (i, j, 0)),
        out_shape=jax.ShapeDtypeStruct((b, t, A_HEADS * A_DHEAD), _BF),
        scratch_shapes=[
            pltpu.VMEM((t, Q_BLOCK), jnp.int32),
            pltpu.VMEM((1, Q_BLOCK), jnp.int32),
            pltpu.VMEM((vt.shape[1], hq), _F32),
            pltpu.VMEM((1, hq), _F32),
            pltpu.VMEM((1, hq), _F32),
            pltpu.VMEM((KEY_TILE, hq), _BF),
        ],
        compiler_params=_params("parallel", "arbitrary"),
        name=name,
    )(ki, k, vt, qit, wi, qt)


def _mlstm_layer(x2, norm_g, w_in, gate_b, head_g, w_out, *, batch, seq):
    n, d = x2.shape
    hk = M_HEADS * M_DQK
    hv = M_HEADS * M_DV
    main = 2 * hk + 2 * hv
    proj = _norm_matmul(x2, norm_g, w_in[:, :main].astype(_BF), tm=1024, tn=1024, name="mlstm_in")
    gates = _gates(x2, norm_g, w_in[:, main:], gate_b, tm=512, name="mlstm_gates")
    gates = gates.reshape(batch, seq, 2 * M_HEADS)
    h = _mlstm_core(proj.reshape(batch, seq, main), gates, gates.transpose(0, 2, 1), head_g, name="mlstm_core")
    return _matmul_resid(h.reshape(n, hv), w_out.astype(_BF), x2, tm=512, tn=d, name="mlstm_out")


def _dsa_layer(x2, norm_g, w_in, q_g, k_g, w_out, *, batch, seq):
    n, d = x2.shape
    qd = A_HEADS * A_DHEAD
    qid = IDX_HEADS * IDX_DIM
    o_k, o_v, o_qi = qd, qd + A_DHEAD, qd + 2 * A_DHEAD
    o_ki, o_wi, o_end = o_qi + qid, o_qi + qid + IDX_DIM, o_qi + qid + IDX_DIM + IDX_HEADS
    tn = 512
    padded = -(-o_end // tn) * tn
    w_pad = jnp.zeros((d, padded), _BF).at[:, :o_end].set(w_in.astype(_BF))
    proj = _norm_matmul(x2, norm_g, w_pad, tm=1024, tn=tn, name="dsa_in")
    qt, qit, kn, ki, vt = _dsa_prep(proj, q_g, k_g, batch=batch, seq=seq, offs=(o_k, o_v, o_qi, o_ki), name="dsa_qknorm")
    nb = seq // Q_BLOCK
    wi = proj[:, o_wi:o_end].reshape(batch, nb, Q_BLOCK, IDX_HEADS).transpose(0, 1, 3, 2).reshape(batch, nb, 1, IDX_HEADS * Q_BLOCK)
    out = _dsa_attend(ki.reshape(batch, seq, IDX_DIM), kn.reshape(batch, seq, A_DHEAD), vt, qit, wi, qt, name="dsa_attend")
    return _matmul_resid(out.reshape(n, qd), w_out.astype(_BF), x2, tm=512, tn=d, name="dsa_out")


def _conv_layer(x2, norm_g, w_in, dw_w, dw_b, ln_g, ln_b, w_out, *, seq):
    u = _norm_glu(x2, norm_g, w_in.astype(_BF), tm=1024, tn=512, name="conv_in")
    v = _conv_mid(u, dw_w, dw_b, ln_g, ln_b, seq=seq, tm=256, name="conv_mid")
    return _matmul_resid(v, w_out.astype(_BF), x2, tm=512, tn=x2.shape[1], name="conv_out")


def _ffn_layer(x2, norm_g, w_up, dw, w_down, *, seq):
    return _ffn(x2, norm_g, w_up.astype(_BF), dw, w_down.astype(_BF), seq=seq, tm=512, tn=512, name="ffn")


def kernel(x, mix_norm_g, ffn_norm_g, mlstm_w_in, mlstm_gate_b, mlstm_head_g, mlstm_w_out,
           dsa_w_in, dsa_q_g, dsa_k_g, dsa_w_out, conv_w_in, conv_dw_w, conv_dw_b,
           conv_ln_g, conv_ln_b, conv_w_out, ffn_w_up, ffn_dw_w, ffn_w_down):
    batch, seq, d = x.shape
    depth = mix_norm_g.shape[0]
    x2 = x.reshape(batch * seq, d)
    for i in range(depth):
        kind = i % N_MIXERS
        j = i // N_MIXERS
        if kind == 0:
            x2 = _mlstm_layer(x2, mix_norm_g[i], mlstm_w_in[j], mlstm_gate_b[j], mlstm_head_g[j], mlstm_w_out[j],
                              batch=batch, seq=seq)
        elif kind == 1:
            x2 = _dsa_layer(x2, mix_norm_g[i], dsa_w_in[j], dsa_q_g[j], dsa_k_g[j], dsa_w_out[j], batch=batch, seq=seq)
        else:
            x2 = _conv_layer(x2, mix_norm_g[i], conv_w_in[j], conv_dw_w[j], conv_dw_b[j], conv_ln_g[j], conv_ln_b[j],
                             conv_w_out[j], seq=seq)
        x2 = _ffn_layer(x2, ffn_norm_g[i], ffn_w_up[i], ffn_dw_w[i], ffn_w_down[i], seq=seq)
    return x2.reshape(batch, seq, d)
```

```python
import functools

import jax
import jax.numpy as jnp
from jax import lax
from jax.experimental import pallas as pl
from jax.experimental.pallas import tpu as pltpu

N_MIXERS = 3
EPS = 1e-6
M_HEADS = 8
M_DQK = 128
M_DV = 256
A_HEADS = 16
A_DHEAD = 128
IDX_HEADS = 16
IDX_DIM = 64
TOPK_MAX = 256
CONV_WIDTH = 31
FFN_CONV_WIDTH = 3

LANES = 128
SUBLANES = 8
FFN_HALO = 16
VMEM_LIMIT = 52 * 1024 * 1024
MLSTM_CHUNK = 256
Q_BLOCK = 128
KEY_TILE = 256
CONV_HALO = 32
LOG2_E = 1.4426950408889634
INT_MIN = -(2 ** 31)
NEG_BIG = -1e30
MAX_FIXED_SHIFT_SPAN = 120.0

_BF = jnp.bfloat16
_F32 = jnp.float32


def _params(*sem):
    return pltpu.CompilerParams(dimension_semantics=sem, vmem_limit_bytes=VMEM_LIMIT)


def _rms_rows(x, g):
    ms = jnp.mean(x * x, axis=-1, keepdims=True)
    return x * lax.rsqrt(ms + EPS) * g


def _dot(a, b):
    return jnp.dot(a, b, preferred_element_type=_F32)


def _split3(x):
    hi = x.astype(_BF)
    r1 = x - hi.astype(_F32)
    mid = r1.astype(_BF)
    lo = (r1 - mid.astype(_F32)).astype(_BF)
    return hi, mid, lo


def _norm_matmul_kernel(x_ref, g_ref, w_ref, o_ref, xn_ref):
    @pl.when(pl.program_id(1) == 0)
    def _():
        xn_ref[...] = _rms_rows(x_ref[...], g_ref[...]).astype(_BF)

    o_ref[...] = _dot(xn_ref[...], w_ref[...]).astype(o_ref.dtype)


def _norm_matmul(x, g, w, *, tm, tn, name):
    n, d = x.shape
    m = w.shape[1]
    return pl.pallas_call(
        _norm_matmul_kernel,
        grid=(n // tm, m // tn),
        in_specs=[
            pl.BlockSpec((tm, d), lambda i, j: (i, 0)),
            pl.BlockSpec((1, d), lambda i, j: (0, 0)),
            pl.BlockSpec((d, tn), lambda i, j: (0, j)),
        ],
        out_specs=pl.BlockSpec((tm, tn), lambda i, j: (i, j)),
        out_shape=jax.ShapeDtypeStruct((n, m), _F32),
        scratch_shapes=[pltpu.VMEM((tm, d), _BF)],
        compiler_params=_params("parallel", "arbitrary"),
        name=name,
    )(x, g.reshape(1, d), w)


def _norm_glu_kernel(x_ref, g_ref, wa_ref, wg_ref, o_ref, xn_ref):
    @pl.when(pl.program_id(1) == 0)
    def _():
        xn_ref[...] = _rms_rows(x_ref[...], g_ref[...]).astype(_BF)

    xn = xn_ref[...]
    a = _dot(xn, wa_ref[...])
    gate = _dot(xn, wg_ref[...])
    o_ref[...] = a * jax.nn.sigmoid(gate)


def _norm_glu(x, g, w, *, tm, tn, name):
    n, d = x.shape
    half = w.shape[1] // 2
    nj = half // tn
    return pl.pallas_call(
        _norm_glu_kernel,
        grid=(n // tm, nj),
        in_specs=[
            pl.BlockSpec((tm, d), lambda i, j: (i, 0)),
            pl.BlockSpec((1, d), lambda i, j: (0, 0)),
            pl.BlockSpec((d, tn), lambda i, j: (0, j)),
            pl.BlockSpec((d, tn), lambda i, j: (0, j + nj)),
        ],
        out_specs=pl.BlockSpec((tm, tn), lambda i, j: (i, j)),
        out_shape=jax.ShapeDtypeStruct((n, half), _F32),
        scratch_shapes=[pltpu.VMEM((tm, d), _BF)],
        compiler_params=_params("parallel", "arbitrary"),
        name=name,
    )(x, g.reshape(1, d), w, w)


def _matmul_resid_kernel(a_ref, w_ref, r_ref, o_ref):
    o_ref[...] = r_ref[...] + _dot(a_ref[...], w_ref[...])


def _matmul_resid(a, w, resid, *, tm, tn, name):
    n, k = a.shape
    m = w.shape[1]
    return pl.pallas_call(
        _matmul_resid_kernel,
        grid=(n // tm, m // tn),
        in_specs=[
            pl.BlockSpec((tm, k), lambda i, j: (i, 0)),
            pl.BlockSpec((k, tn), lambda i, j: (0, j)),
            pl.BlockSpec((tm, tn), lambda i, j: (i, j)),
        ],
        out_specs=pl.BlockSpec((tm, tn), lambda i, j: (i, j)),
        out_shape=jax.ShapeDtypeStruct((n, m), _F32),
        compiler_params=_params("parallel", "arbitrary"),
        name=name,
    )(a, w, resid)


def _ffn_kernel(x_ref, xh_ref, g_ref, wa_ref, wb_ref, dwa_ref, dwb_ref, wd_ref, o_ref,
                xn_ref, ha0_ref, hb0_ref, ha1_ref, hb1_ref, act_ref, *, tm, nj, total, tiles_per_seq, act_rows):
    s = pl.program_id(0)

    @pl.when(s == 0)
    def _():
        ha1_ref[...] = jnp.zeros_like(ha1_ref)
        hb1_ref[...] = jnp.zeros_like(hb1_ref)
        o_ref[...] = jnp.zeros_like(o_ref)

    @pl.when(jnp.logical_and(s % nj == 0, s < total))
    def _():
        g = g_ref[...]
        halo = _rms_rows(xh_ref[...], g)
        halo = jnp.where((s // nj) % tiles_per_seq == 0, 0.0, halo)
        xn_ref[0:FFN_HALO, :] = halo.astype(_BF)
        xn_ref[FFN_HALO:, :] = _rms_rows(x_ref[...], g).astype(_BF)

    @pl.when((s - 1) % nj == 0)
    def _():
        o_ref[...] = x_ref[...]

    def conv(h_ref, dw, r0, rows):
        base = r0 + FFN_HALO - (FFN_CONV_WIDTH - 1)
        out = dw[0:1, :] * h_ref[base:base + rows, :]
        for k in range(1, FFN_CONV_WIDTH):
            out = out + dw[k:k + 1, :] * h_ref[base + k:base + k + rows, :]
        return out

    def step(ha_prev, hb_prev, ha_next, hb_next):
        ha_next[...] = _dot(xn_ref[...], wa_ref[...])
        hb_next[...] = _dot(xn_ref[...], wb_ref[...])
        dwa = dwa_ref[...]
        dwb = dwb_ref[...]
        for r0 in range(0, tm, act_rows):
            a = conv(ha_prev, dwa, r0, act_rows)
            b = conv(hb_prev, dwb, r0, act_rows)
            act_ref[r0:r0 + act_rows, :] = (a * jax.nn.sigmoid(a) * b).astype(_BF)
        o_ref[...] += _dot(act_ref[...], wd_ref[...])

    @pl.when(s % 2 == 0)
    def _():
        step(ha1_ref, hb1_ref, ha0_ref, hb0_ref)

    @pl.when(s % 2 == 1)
    def _():
        step(ha0_ref, hb0_ref, ha1_ref, hb1_ref)


def _ffn(x, g, w_up, dw, w_down, *, seq, tm, tn, name):
    n, d = x.shape
    f = w_up.shape[1] // 2
    nj = f // tn
    total = (n // tm) * nj
    hblk = tm // FFN_HALO
    kern = functools.partial(_ffn_kernel, tm=tm, nj=nj, total=total, tiles_per_seq=seq // tm, act_rows=64)

    def cur(s):
        c = jnp.minimum(s, total - 1)
        return c // nj, c % nj

    def prev(s):
        p = jnp.maximum(s - 1, 0)
        return p // nj, p % nj

    h_scratch = pltpu.VMEM((tm + FFN_HALO, tn), _F32)
    return pl.pallas_call(
        kern,
        grid=(total + 1,),
        in_specs=[
            pl.BlockSpec((tm, d), lambda s: (cur(s)[0], 0)),
            pl.BlockSpec((FFN_HALO, d), lambda s: (jnp.maximum(cur(s)[0] * hblk - 1, 0), 0)),
            pl.BlockSpec((1, d), lambda s: (0, 0)),
            pl.BlockSpec((d, tn), lambda s: (0, cur(s)[1])),
            pl.BlockSpec((d, tn), lambda s: (0, cur(s)[1] + nj)),
            pl.BlockSpec((FFN_CONV_WIDTH, tn), lambda s: (0, prev(s)[1])),
            pl.BlockSpec((FFN_CONV_WIDTH, tn), lambda s: (0, prev(s)[1] + nj)),
            pl.BlockSpec((tn, d), lambda s: (prev(s)[1], 0)),
        ],
        out_specs=pl.BlockSpec((tm, d), lambda s: (prev(s)[0], 0)),
        out_shape=jax.ShapeDtypeStruct((n, d), _F32),
        scratch_shapes=[pltpu.VMEM((tm + FFN_HALO, d), _BF), h_scratch, h_scratch, h_scratch, h_scratch,
                        pltpu.VMEM((tm, tn), _BF)],
        compiler_params=_params("arbitrary"),
        name=name,
    )(x, x, g.reshape(1, d), w_up, w_up, dw, dw, w_down)


def _conv_mid_kernel(u_ref, uh_ref, dw_ref, db_ref, lg_ref, lb_ref, o_ref, ext_ref, c_ref, sh_ref, *, tm, tiles_per_seq, rb, cb):
    i = pl.program_id(0)
    d = u_ref.shape[1]
    ext_ref[0:CONV_HALO, :] = jnp.where(i % tiles_per_seq == 0, 0.0, uh_ref[...])
    ext_ref[CONV_HALO:, :] = u_ref[...]
    base = CONV_HALO - (CONV_WIDTH - 1)

    def col_block(c, carry):
        cs = pl.multiple_of(c * cb, cb)
        for r in range(tm // rb):
            acc = jnp.zeros((rb, cb), _F32) + db_ref[:, pl.ds(cs, cb)]
            for phase in range(SUBLANES):
                taps = [k for k in range(CONV_WIDTH) if (base + k) % SUBLANES == phase]
                span = rb + max(base + k - phase for k in taps)
                sh_ref[0:span, :] = ext_ref[r * rb + phase:r * rb + phase + span, pl.ds(cs, cb)]
                for k in taps:
                    off = base + k - phase
                    acc = acc + sh_ref[off:off + rb, :] * dw_ref[k:k + 1, pl.ds(cs, cb)]
            c_ref[r * rb:(r + 1) * rb, pl.ds(cs, cb)] = acc
        return carry

    lax.fori_loop(0, d // cb, col_block, 0)
    c = c_ref[...]
    mu = jnp.mean(c, axis=-1, keepdims=True)
    xc = c - mu
    var = jnp.mean(xc * xc, axis=-1, keepdims=True)
    y = xc * lax.rsqrt(var + EPS) * lg_ref[...] + lb_ref[...]
    o_ref[...] = (y * jax.nn.sigmoid(y)).astype(o_ref.dtype)


def _conv_mid(u, dw, db, lg, lb, *, seq, tm, name):
    n, d = u.shape
    hblk = tm // CONV_HALO
    rb, cb = 64, 256
    kern = functools.partial(_conv_mid_kernel, tm=tm, tiles_per_seq=seq // tm, rb=rb, cb=cb)
    return pl.pallas_call(
        kern,
        grid=(n // tm,),
        in_specs=[
            pl.BlockSpec((tm, d), lambda i: (i, 0)),
            pl.BlockSpec((CONV_HALO, d), lambda i: (jnp.maximum(i * hblk - 1, 0), 0)),
            pl.BlockSpec((CONV_WIDTH, d), lambda i: (0, 0)),
            pl.BlockSpec((1, d), lambda i: (0, 0)),
            pl.BlockSpec((1, d), lambda i: (0, 0)),
            pl.BlockSpec((1, d), lambda i: (0, 0)),
        ],
        out_specs=pl.BlockSpec((tm, d), lambda i: (i, 0)),
        out_shape=jax.ShapeDtypeStruct((n, d), _BF),
        scratch_shapes=[pltpu.VMEM((tm + CONV_HALO, d), _F32), pltpu.VMEM((tm, d), _F32),
                        pltpu.VMEM((rb + CONV_HALO, cb), _F32)],
        compiler_params=_params("parallel"),
        name=name,
    )(u, u, dw, db.reshape(1, d), lg.reshape(1, d), lb.reshape(1, d))


def _gates_kernel(x_ref, g_ref, whi_ref, wlo_ref, b_ref, o_ref):
    xn = _rms_rows(x_ref[...], g_ref[...])
    xhi = xn.astype(_BF)
    xlo = (xn - xhi.astype(_F32)).astype(_BF)
    whi = whi_ref[...]
    z = _dot(xhi, whi) + _dot(xlo, whi) + _dot(xhi, wlo_ref[...]) + b_ref[...]
    log_f = jnp.minimum(z, 0.0) - jnp.log(1.0 + jnp.exp(-jnp.abs(z)))
    col = lax.broadcasted_iota(jnp.int32, z.shape, 1)
    o_ref[...] = jnp.where(col < M_HEADS, z, log_f)


def _gates(x, g, w_gate, bias, *, tm, name):
    n, d = x.shape
    ng = w_gate.shape[1]
    wp = jnp.zeros((d, LANES), _F32).at[:, :ng].set(w_gate)
    whi = wp.astype(_BF)
    wlo = (wp - whi.astype(_F32)).astype(_BF)
    bp = jnp.zeros((1, LANES), _F32).at[0, :ng].set(bias)
    out = pl.pallas_call(
        _gates_kernel,
        grid=(n // tm,),
        in_specs=[
            pl.BlockSpec((tm, d), lambda i: (i, 0)),
            pl.BlockSpec((1, d), lambda i: (0, 0)),
            pl.BlockSpec((d, LANES), lambda i: (0, 0)),
            pl.BlockSpec((d, LANES), lambda i: (0, 0)),
            pl.BlockSpec((1, LANES), lambda i: (0, 0)),
        ],
        out_specs=pl.BlockSpec((tm, LANES), lambda i: (i, 0)),
        out_shape=jax.ShapeDtypeStruct((n, LANES), _F32),
        compiler_params=_params("parallel"),
        name=name,
    )(x, g.reshape(1, d), whi, wlo, bp)
    return out[:, :ng]


def _mlstm_kernel(q_ref, k_ref, v_ref, o_ref, gc_ref, gr_ref, hg_ref, out_ref, c_ref, n_ref, m_ref, *, chunk):
    ln = chunk
    scale = M_DQK ** -0.5

    @pl.when(pl.program_id(1) == 0)
    def _():
        c_ref[...] = jnp.zeros_like(c_ref)
        n_ref[...] = jnp.zeros_like(n_ref)
        m_ref[...] = jnp.zeros_like(m_ref)

    gc = gc_ref[0]
    gr = gr_ref[0]
    rows = lax.broadcasted_iota(jnp.int32, (ln, ln), 0)
    cols = lax.broadcasted_iota(jnp.int32, (ln, ln), 1)
    causal = rows >= cols
    tril = causal.astype(_BF)
    triu = (rows <= cols).astype(_BF)
    cum_col = sum(_dot(tril, p) for p in _split3(gc))
    cum_row = sum(_dot(p, triu) for p in _split3(gr))

    for h in range(M_HEADS):
        f = M_HEADS + h
        bcol = cum_col[:, f:f + 1]
        brow = cum_row[f:f + 1, :]
        icol = gc[:, h:h + 1]
        irow = gr[h:h + 1, :]
        m_prev = m_ref[h, 0:1, 0:1]
        nrow = n_ref[h, 0:1, :]
        qh = q_ref[0, :, h * M_DQK:(h + 1) * M_DQK]
        kh = k_ref[0, :, h * M_DQK:(h + 1) * M_DQK]
        vb = v_ref[0, :, h * M_DV:(h + 1) * M_DV].astype(_BF)
        qb = qh.astype(_BF)

        dmat = jnp.where(causal, bcol - brow + irow, -jnp.inf)
        inter = bcol + m_prev
        m_t = jnp.maximum(inter, jnp.max(dmat, axis=1, keepdims=True))
        w_intra = jnp.exp(dmat - m_t)
        w_inter = jnp.exp(inter - m_t)
        s = lax.dot_general(qb, kh.astype(_BF), (((1,), (1,)), ((), ())), preferred_element_type=_F32)
        s = s * (w_intra * scale)
        qc = _dot(qb, c_ref[h].astype(_BF)) * scale
        num = w_inter * qc + _dot(s.astype(_BF), vb)
        qn = jnp.sum(qh * nrow, axis=1, keepdims=True) * scale
        den = w_inter * qn + jnp.sum(s, axis=1, keepdims=True)
        hh = num / jnp.maximum(jnp.abs(den), jnp.exp(-m_t))

        bl = bcol[ln - 1:ln, :]
        g_col = bl - bcol + icol
        g_row = bl - brow + irow
        m_new = jnp.maximum(bl + m_prev, jnp.max(g_row, axis=1, keepdims=True))
        a_prev = jnp.exp(bl + m_prev - m_new)
        kw = kh * jnp.exp(g_col - m_new)
        upd = lax.dot_general(kw.astype(_BF), vb, (((0,), (0,)), ((), ())), preferred_element_type=_F32)
        c_ref[h] = a_prev * c_ref[h] + upd
        n_new = a_prev * nrow + jnp.sum(kw, axis=0, keepdims=True)
        n_ref[h] = jnp.broadcast_to(n_new, (SUBLANES, M_DQK))
        m_ref[h] = jnp.broadcast_to(m_new, (SUBLANES, LANES))

        hn = hh * lax.rsqrt(jnp.mean(hh * hh, axis=1, keepdims=True) + EPS) * hg_ref[:, h * M_DV:(h + 1) * M_DV]
        og = o_ref[0, :, h * M_DV:(h + 1) * M_DV]
        out_ref[0, :, h * M_DV:(h + 1) * M_DV] = (hn * jax.nn.sigmoid(og)).astype(out_ref.dtype)


def _mlstm_core(proj, gates_col, gates_row, head_g, *, name):
    b, t, _ = proj.shape
    hk = M_HEADS * M_DQK
    hv = M_HEADS * M_DV
    ln = MLSTM_CHUNK
    ng = 2 * M_HEADS
    kern = functools.partial(_mlstm_kernel, chunk=ln)
    return pl.pallas_call(
        kern,
        grid=(b, t // ln),
        in_specs=[
            pl.BlockSpec((1, ln, hk), lambda i, c: (i, c, 0)),
            pl.BlockSpec((1, ln, hk), lambda i, c: (i, c, 1)),
            pl.BlockSpec((1, ln, hv), lambda i, c: (i, c, (2 * hk) // hv)),
            pl.BlockSpec((1, ln, hv), lambda i, c: (i, c, (2 * hk) // hv + 1)),
            pl.BlockSpec((1, ln, ng), lambda i, c: (i, c, 0)),
            pl.BlockSpec((1, ng, ln), lambda i, c: (i, 0, c)),
            pl.BlockSpec((1, hv), lambda i, c: (0, 0)),
        ],
        out_specs=pl.BlockSpec((1, ln, hv), lambda i, c: (i, c, 0)),
        out_shape=jax.ShapeDtypeStruct((b, t, hv), _BF),
        scratch_shapes=[
            pltpu.VMEM((M_HEADS, M_DQK, M_DV), _F32),
            pltpu.VMEM((M_HEADS, SUBLANES, M_DQK), _F32),
            pltpu.VMEM((M_HEADS, SUBLANES, LANES), _F32),
        ],
        compiler_params=_params("parallel", "arbitrary"),
        name=name,
    )(proj, proj, proj, proj, gates_col, gates_row, head_g.reshape(1, hv))


def _dsa_prep_kernel(p_ref, qg_ref, kg_ref, qt_ref, qit_ref, k_ref, ki_ref, vt_ref, *, offs):
    o_k, o_v, o_qi, o_ki = offs
    qg = qg_ref[...]
    scale = A_DHEAD ** -0.5 * LOG2_E
    for h in range(A_HEADS):
        sl = slice(h * A_DHEAD, (h + 1) * A_DHEAD)
        qh = _rms_rows(p_ref[:, sl], qg) * scale
        qt_ref[0, 0, :, h * Q_BLOCK:(h + 1) * Q_BLOCK] = qh.T.astype(_BF)
    pair = LANES // IDX_DIM
    for g in range(IDX_HEADS // pair):
        slab = p_ref[:, o_qi + g * LANES:o_qi + (g + 1) * LANES].T
        for e in range(pair):
            h = g * pair + e
            qit_ref[0, 0, :, h * Q_BLOCK:(h + 1) * Q_BLOCK] = slab[e * IDX_DIM:(e + 1) * IDX_DIM, :].astype(_BF)
    k_ref[...] = _rms_rows(p_ref[:, o_k:o_v], kg_ref[...]).astype(_BF)
    ki_ref[...] = p_ref[:, o_ki:o_ki + IDX_DIM].astype(_BF)
    vt_ref[0, 0:A_DHEAD, :] = p_ref[:, o_v:o_qi].T.astype(_BF)
    ones_row = lax.broadcasted_iota(jnp.int32, (SUBLANES, Q_BLOCK), 0) == 0
    vt_ref[0, A_DHEAD:, :] = ones_row.astype(_F32).astype(_BF)


def _dsa_prep(proj, q_g, k_g, *, batch, seq, offs, name):
    n, width = proj.shape
    nb = seq // Q_BLOCK
    hq = A_HEADS * Q_BLOCK
    kern = functools.partial(_dsa_prep_kernel, offs=offs)
    return pl.pallas_call(
        kern,
        grid=(batch, nb),
        in_specs=[
            pl.BlockSpec((Q_BLOCK, width), lambda i, j: (i * nb + j, 0)),
            pl.BlockSpec((1, A_DHEAD), lambda i, j: (0, 0)),
            pl.BlockSpec((1, A_DHEAD), lambda i, j: (0, 0)),
        ],
        out_specs=[
            pl.BlockSpec((1, 1, A_DHEAD, hq), lambda i, j: (i, j, 0, 0)),
            pl.BlockSpec((1, 1, IDX_DIM, hq), lambda i, j: (i, j, 0, 0)),
            pl.BlockSpec((Q_BLOCK, A_DHEAD), lambda i, j: (i * nb + j, 0)),
            pl.BlockSpec((Q_BLOCK, IDX_DIM), lambda i, j: (i * nb + j, 0)),
            pl.BlockSpec((1, A_DHEAD + SUBLANES, Q_BLOCK), lambda i, j: (i, 0, j)),
        ],
        out_shape=[
            jax.ShapeDtypeStruct((batch, nb, A_DHEAD, hq), _BF),
            jax.ShapeDtypeStruct((batch, nb, IDX_DIM, hq), _BF),
            jax.ShapeDtypeStruct((n, A_DHEAD), _BF),
            jax.ShapeDtypeStruct((n, IDX_DIM), _BF),
            jax.ShapeDtypeStruct((batch, A_DHEAD + SUBLANES, seq), _BF),
        ],
        compiler_params=_params("parallel", "parallel"),
        name=name,
    )(proj, q_g.reshape(1, A_DHEAD), k_g.reshape(1, A_DHEAD))


def _dsa_kernel(bound_ref, ki_ref, k_ref, vt_ref, qit_ref, wi_ref, qt_ref, o_ref,
                key_ref, hi_ref, lo_ref, cut_ref, acc_ref, m_ref, alpha_ref, p_ref, *, tk, topk):
    q0 = pl.program_id(1) * Q_BLOCK
    n_kt = (q0 + Q_BLOCK - 1) // tk + 1
    n_bits_idx = (key_ref.shape[0] - 1).bit_length()
    lane_q = q0 + lax.broadcasted_iota(jnp.int32, (tk, Q_BLOCK), 1)
    row = lax.broadcasted_iota(jnp.int32, (tk, Q_BLOCK), 0)

    qit = qit_ref[0, 0]
    wi = wi_ref[0, 0] * ((IDX_HEADS * IDX_DIM) ** -0.5)

    def score_tile(kt, carry):
        start = pl.multiple_of(kt * tk, tk)
        lg = _dot(ki_ref[0, pl.ds(start, tk), :], qit)
        w = jnp.maximum(lg, 0.0) * wi
        s = w[:, 0:Q_BLOCK]
        for h in range(1, IDX_HEADS):
            s = s + w[:, h * Q_BLOCK:(h + 1) * Q_BLOCK]
        bits = pltpu.bitcast(s, jnp.int32)
        key = bits ^ ((bits >> 31) & 0x7FFFFFFF)
        key = jnp.where(bits == INT_MIN, 0, key)
        key = jnp.where(start + row <= lane_q, key, INT_MIN)
        key_ref[pl.ds(start, tk), :] = key
        return carry

    lax.fori_loop(0, n_kt, score_tile, 0)

    ck = 2 * tk
    n_ct = (n_kt + 1) // 2

    @pl.when(n_kt % 2 == 1)
    def _():
        key_ref[pl.ds(pl.multiple_of(n_kt * tk, tk), tk), :] = jnp.full((tk, Q_BLOCK), INT_MIN, jnp.int32)

    crow = lax.broadcasted_iota(jnp.int32, (ck, Q_BLOCK), 0)

    def count(pred):
        def body(ct, acc):
            start = pl.multiple_of(ct * ck, ck)
            hit = pred(key_ref[pl.ds(start, ck), :], start + crow).astype(jnp.int32)
            return acc + jnp.sum(hit.reshape(ck // SUBLANES, SUBLANES, Q_BLOCK), axis=0)
        acc = lax.fori_loop(0, n_ct, body, jnp.zeros((SUBLANES, Q_BLOCK), jnp.int32))
        return jnp.sum(acc, axis=0, keepdims=True)

    half = 1 << 15

    def split_tile(ct, carry):
        start = pl.multiple_of(ct * ck, ck)
        keys = key_ref[pl.ds(start, ck), :]
        hi_ref[pl.ds(start, ck), :] = (keys >> 16).astype(jnp.int16)
        lo_ref[pl.ds(start, ck), :] = ((keys & 0xFFFF) - half).astype(jnp.int16)
        return carry

    lax.fori_loop(0, n_ct, split_tile, 0)

    def count16(ref, cand):
        def body(ct, acc):
            start = pl.multiple_of(ct * ck, ck)
            hit = (ref[pl.ds(start, ck), :] >= cand).astype(jnp.int16)
            parts = [hit[r0:r0 + 2 * SUBLANES, :] for r0 in range(0, ck, 2 * SUBLANES)]
            while len(parts) > 1:
                parts = [a + b for a, b in zip(parts[0::2], parts[1::2])]
            return acc + parts[0]
        acc = lax.fori_loop(0, n_ct, body, jnp.zeros((2 * SUBLANES, Q_BLOCK), jnp.int16))
        return jnp.sum(acc.astype(jnp.int32), axis=0, keepdims=True)

    def search16(ref, base_count):
        def bit(b, v):
            cand = v | lax.shift_left(jnp.int32(1), 15 - b)
            cnt = base_count + count16(ref, (cand - half).astype(jnp.int16))
            return jnp.where(cnt >= topk, cand, v)
        return lax.fori_loop(0, 16, bit, jnp.zeros((1, Q_BLOCK), jnp.int32))

    thr_hi = search16(hi_ref, 0) - half
    thr_hi16 = thr_hi.astype(jnp.int16)
    above = count16(hi_ref, (thr_hi + 1).astype(jnp.int16))
    above = jnp.where(thr_hi == half - 1, 0, above)

    def mask_lo_tile(ct, carry):
        start = pl.multiple_of(ct * ck, ck)
        same = hi_ref[pl.ds(start, ck), :] == thr_hi16
        lo_ref[pl.ds(start, ck), :] = jnp.where(same, lo_ref[pl.ds(start, ck), :], jnp.int16(-half))
        return carry

    lax.fori_loop(0, n_ct, mask_lo_tile, 0)
    thr_lo = search16(lo_ref, above)
    thr = (thr_hi << 16) | thr_lo
    n_gt = count(lambda keys, idx: keys > thr)
    n_eq = count(lambda keys, idx: keys == thr)
    need = topk - n_gt
    cut_ref[...] = jnp.full((1, Q_BLOCK), 2 ** 30, jnp.int32)
    tied = jnp.logical_and(n_eq != need, thr != INT_MIN)

    @pl.when(jnp.max(tied.astype(jnp.int32)) > 0)
    def _():
        def cut_bit(b, p):
            cand = p | lax.shift_left(jnp.int32(1), n_bits_idx - 1 - b)
            cnt = count(lambda keys, idx: jnp.logical_and(keys == thr, idx < cand))
            return jnp.where(cnt < need, cand, p)
        cut_ref[...] = lax.fori_loop(0, n_bits_idx, cut_bit, jnp.zeros((1, Q_BLOCK), jnp.int32))

    cut = cut_ref[...]

    acc_ref[...] = jnp.zeros(acc_ref.shape, _F32)
    qt = qt_ref[0, 0]
    bound = bound_ref[0, 0]
    fixed_shift = 2.0 * bound <= MAX_FIXED_SHIFT_SPAN

    def selected(kt):
        start = pl.multiple_of(kt * tk, tk)
        keys = key_ref[pl.ds(start, tk), :]
        sel = jnp.logical_or(keys > thr, jnp.logical_and(keys == thr, start + row <= cut))
        return start, jnp.logical_and(sel, keys != INT_MIN)

    @pl.when(fixed_shift)
    def _():
        def att_tile(kt, carry):
            start, sel = selected(kt)
            mk = jnp.where(sel, -bound, NEG_BIG)
            lg = _dot(k_ref[0, pl.ds(start, tk), :], qt)
            for h in range(A_HEADS):
                sl = slice(h * Q_BLOCK, (h + 1) * Q_BLOCK)
                p_ref[:, sl] = jnp.exp2(lg[:, sl] + mk).astype(_BF)
            acc_ref[...] += _dot(vt_ref[0, :, pl.ds(start, tk)], p_ref[...])
            return carry

        lax.fori_loop(0, n_kt, att_tile, 0)

    @pl.when(jnp.logical_not(fixed_shift))
    def _():
        m_ref[...] = jnp.full(m_ref.shape, NEG_BIG, _F32)

        def att_tile(kt, carry):
            start, sel = selected(kt)
            mk = jnp.where(sel, 0.0, NEG_BIG)
            lg = _dot(k_ref[0, pl.ds(start, tk), :], qt)
            for h in range(A_HEADS):
                sl = slice(h * Q_BLOCK, (h + 1) * Q_BLOCK)
                x = lg[:, sl] + mk
                m_old = m_ref[:, sl]
                m_new = jnp.maximum(m_old, jnp.max(x, axis=0, keepdims=True))
                p_ref[:, sl] = jnp.exp2(x - m_new).astype(_BF)
                alpha_ref[:, sl] = jnp.exp2(m_old - m_new)
                m_ref[:, sl] = m_new
            pv = _dot(vt_ref[0, :, pl.ds(start, tk)], p_ref[...])
            acc_ref[...] = acc_ref[...] * alpha_ref[...] + pv
            return carry

        lax.fori_loop(0, n_kt, att_tile, 0)

    for h in range(A_HEADS):
        sl = slice(h * Q_BLOCK, (h + 1) * Q_BLOCK)
        out_h = acc_ref[0:A_DHEAD, sl] / acc_ref[A_DHEAD:A_DHEAD + 1, sl]
        o_ref[0, :, h * A_DHEAD:(h + 1) * A_DHEAD] = out_h.T.astype(o_ref.dtype)


def _dsa_attend(bound, ki, k, vt, qit, wi, qt, *, name):
    b, t, _ = ki.shape
    nb = t // Q_BLOCK
    hq = A_HEADS * Q_BLOCK
    topk = min(TOPK_MAX, t // 4)
    kern = functools.partial(_dsa_kernel, tk=KEY_TILE, topk=topk)
    return pl.pallas_call(
        kern,
        grid=(b, nb),
        in_specs=[
            pl.BlockSpec(memory_space=pltpu.SMEM),
            pl.BlockSpec((1, t, IDX_DIM), lambda i, j: (i, 0, 0)),
            pl.BlockSpec((1, t, A_DHEAD), lambda i, j: (i, 0, 0)),
            pl.BlockSpec((1, vt.shape[1], t), lambda i, j: (i, 0, 0)),
            pl.BlockSpec((1, 1, IDX_DIM, hq), lambda i, j: (i, j, 0, 0)),
            pl.BlockSpec((1, 1, 1, hq), lambda i, j: (i, j, 0, 0)),
            pl.BlockSpec((1, 1, A_DHEAD, hq), lambda i, j: (i, j, 0, 0)),
        ],
        out_specs=pl.BlockSpec((1, Q_BLOCK, A_HEADS * A_DHEAD), lambda i, j: (i, j, 0)),
        out_shape=jax.ShapeDtypeStruct((b, t, A_HEADS * A_DHEAD), _BF),
        scratch_shapes=[
            pltpu.VMEM((t, Q_BLOCK), jnp.int32),
            pltpu.VMEM((t, Q_BLOCK), jnp.int16),
            pltpu.VMEM((t, Q_BLOCK), jnp.int16),
            pltpu.VMEM((1, Q_BLOCK), jnp.int32),
            pltpu.VMEM((vt.shape[1], hq), _F32),
            pltpu.VMEM((1, hq), _F32),
            pltpu.VMEM((1, hq), _F32),
            pltpu.VMEM((KEY_TILE, hq), _BF),
        ],
        compiler_params=_params("parallel", "arbitrary"),
        name=name,
    )(bound.reshape(1, 1), ki, k, vt, qit, wi, qt)


def _mlstm_layer(x2, norm_g, w_in, gate_b, head_g, w_out, *, batch, seq):
    n, d = x2.shape
    hk = M_HEADS * M_DQK
    hv = M_HEADS * M_DV
    main = 2 * hk + 2 * hv
    proj = _norm_matmul(x2, norm_g, w_in[:, :main].astype(_BF), tm=1024, tn=1024, name="mlstm_in")
    gates = _gates(x2, norm_g, w_in[:, main:], gate_b, tm=512, name="mlstm_gates")
    gates = gates.reshape(batch, seq, 2 * M_HEADS)
    h = _mlstm_core(proj.reshape(batch, seq, main), gates, gates.transpose(0, 2, 1), head_g, name="mlstm_core")
    return _matmul_resid(h.reshape(n, hv), w_out.astype(_BF), x2, tm=512, tn=d, name="mlstm_out")


def _dsa_layer(x2, norm_g, w_in, q_g, k_g, w_out, *, batch, seq):
    n, d = x2.shape
    qd = A_HEADS * A_DHEAD
    qid = IDX_HEADS * IDX_DIM
    o_k, o_v, o_qi = qd, qd + A_DHEAD, qd + 2 * A_DHEAD
    o_ki, o_wi, o_end = o_qi + qid, o_qi + qid + IDX_DIM, o_qi + qid + IDX_DIM + IDX_HEADS
    tn = 512
    padded = -(-o_end // tn) * tn
    w_pad = jnp.zeros((d, padded), _BF).at[:, :o_end].set(w_in.astype(_BF))
    proj = _norm_matmul(x2, norm_g, w_pad, tm=1024, tn=tn, name="dsa_in")
    qt, qit, kn, ki, vt = _dsa_prep(proj, q_g, k_g, batch=batch, seq=seq, offs=(o_k, o_v, o_qi, o_ki), name="dsa_qknorm")
    nb = seq // Q_BLOCK
    wi = proj[:, o_wi:o_end].reshape(batch, nb, Q_BLOCK, IDX_HEADS).transpose(0, 1, 3, 2).reshape(batch, nb, 1, IDX_HEADS * Q_BLOCK)
    bound = 1.01 * LOG2_E * A_DHEAD ** 0.5 * jnp.max(jnp.abs(q_g)) * jnp.max(jnp.abs(k_g))
    out = _dsa_attend(bound.astype(_F32), ki.reshape(batch, seq, IDX_DIM), kn.reshape(batch, seq, A_DHEAD), vt, qit, wi, qt, name="dsa_attend")
    return _matmul_resid(out.reshape(n, qd), w_out.astype(_BF), x2, tm=512, tn=d, name="dsa_out")


def _conv_layer(x2, norm_g, w_in, dw_w, dw_b, ln_g, ln_b, w_out, *, seq):
    u = _norm_glu(x2, norm_g, w_in.astype(_BF), tm=1024, tn=512, name="conv_in")
    v = _conv_mid(u, dw_w, dw_b, ln_g, ln_b, seq=seq, tm=256, name="conv_mid")
    return _matmul_resid(v, w_out.astype(_BF), x2, tm=512, tn=x2.shape[1], name="conv_out")


def _ffn_layer(x2, norm_g, w_up, dw, w_down, *, seq):
    return _ffn(x2, norm_g, w_up.astype(_BF), dw, w_down.astype(_BF), seq=seq, tm=512, tn=512, name="ffn")


def kernel(x, mix_norm_g, ffn_norm_g, mlstm_w_in, mlstm_gate_b, mlstm_head_g, mlstm_w_out,
           dsa_w_in, dsa_q_g, dsa_k_g, dsa_w_out, conv_w_in, conv_dw_w, conv_dw_b,
           conv_ln_g, conv_ln_b, conv_w_out, ffn_w_up, ffn_dw_w, ffn_w_down):
    batch, seq, d = x.shape
    depth = mix_norm_g.shape[0]
    x2 = x.reshape(batch * seq, d)
    for i in range(depth):
        kind = i % N_MIXERS
        j = i // N_MIXERS
        if kind == 0:
            x2 = _mlstm_layer(x2, mix_norm_g[i], mlstm_w_in[j], mlstm_gate_b[j], mlstm_head_g[j], mlstm_w_out[j],
                              batch=batch, seq=seq)
        elif kind == 1:
            x2 = _dsa_layer(x2, mix_norm_g[i], dsa_w_in[j], dsa_q_g[j], dsa_k_g[j], dsa_w_out[j], batch=batch, seq=seq)
        else:
            x2 = _conv_layer(x2, mix_norm_g[i], conv_w_in[j], conv_dw_w[j], conv_dw_b[j], conv_ln_g[j], conv_ln_b[j],
                             conv_w_out[j], seq=seq)
        x2 = _ffn_layer(x2, ffn_norm_g[i], ffn_w_up[i], ffn_dw_w[i], ffn_w_down[i], seq=seq)
    return x2.reshape(batch, seq, d)
```

```python
import functools

import jax
import jax.numpy as jnp
from jax import lax
from jax.experimental import pallas as pl
from jax.experimental.pallas import tpu as pltpu

N_MIXERS = 3
EPS = 1e-6
M_HEADS = 8
M_DQK = 128
M_DV = 256
A_HEADS = 16
A_DHEAD = 128
IDX_HEADS = 16
IDX_DIM = 64
TOPK_MAX = 256
CONV_WIDTH = 31
FFN_CONV_WIDTH = 3

LANES = 128
SUBLANES = 8
FFN_HALO = 16
VMEM_LIMIT = 52 * 1024 * 1024
MLSTM_CHUNK = 256
Q_BLOCK = 128
KEY_TILE = 256
CONV_HALO = 32
LOG2_E = 1.4426950408889634
INT_MIN = -(2 ** 31)
NEG_BIG = -1e30
MAX_FIXED_SHIFT_SPAN = 120.0

_BF = jnp.bfloat16
_F32 = jnp.float32


def _params(*sem):
    return pltpu.CompilerParams(dimension_semantics=sem, vmem_limit_bytes=VMEM_LIMIT)


def _rms_rows(x, g):
    ms = jnp.mean(x * x, axis=-1, keepdims=True)
    return x * lax.rsqrt(ms + EPS) * g


def _dot(a, b):
    return jnp.dot(a, b, preferred_element_type=_F32)


def _split3(x):
    hi = x.astype(_BF)
    r1 = x - hi.astype(_F32)
    mid = r1.astype(_BF)
    lo = (r1 - mid.astype(_F32)).astype(_BF)
    return hi, mid, lo


def _norm_matmul_kernel(x_ref, g_ref, w_ref, o_ref, xn_ref):
    @pl.when(pl.program_id(1) == 0)
    def _():
        xn_ref[...] = _rms_rows(x_ref[...], g_ref[...]).astype(_BF)

    o_ref[...] = _dot(xn_ref[...], w_ref[...]).astype(o_ref.dtype)


def _norm_matmul(x, g, w, *, layer, cols, tm, tn, name):
    n, d = x.shape
    m = cols
    return pl.pallas_call(
        _norm_matmul_kernel,
        grid=(n // tm, m // tn),
        in_specs=[
            pl.BlockSpec((tm, d), lambda i, j: (i, 0)),
            pl.BlockSpec((1, d), lambda i, j: (0, 0)),
            pl.BlockSpec((None, d, tn), lambda i, j: (layer, 0, j)),
        ],
        out_specs=pl.BlockSpec((tm, tn), lambda i, j: (i, j)),
        out_shape=jax.ShapeDtypeStruct((n, m), _F32),
        scratch_shapes=[pltpu.VMEM((tm, d), _BF)],
        compiler_params=_params("parallel", "arbitrary"),
        name=name,
    )(x, g.reshape(1, d), w)


def _norm_glu_kernel(x_ref, g_ref, wa_ref, wg_ref, o_ref, xn_ref):
    @pl.when(pl.program_id(1) == 0)
    def _():
        xn_ref[...] = _rms_rows(x_ref[...], g_ref[...]).astype(_BF)

    xn = xn_ref[...]
    a = _dot(xn, wa_ref[...])
    gate = _dot(xn, wg_ref[...])
    o_ref[...] = a * jax.nn.sigmoid(gate)


def _norm_glu(x, g, w, *, tm, tn, name):
    n, d = x.shape
    half = w.shape[1] // 2
    nj = half // tn
    return pl.pallas_call(
        _norm_glu_kernel,
        grid=(n // tm, nj),
        in_specs=[
            pl.BlockSpec((tm, d), lambda i, j: (i, 0)),
            pl.BlockSpec((1, d), lambda i, j: (0, 0)),
            pl.BlockSpec((d, tn), lambda i, j: (0, j)),
            pl.BlockSpec((d, tn), lambda i, j: (0, j + nj)),
        ],
        out_specs=pl.BlockSpec((tm, tn), lambda i, j: (i, j)),
        out_shape=jax.ShapeDtypeStruct((n, half), _F32),
        scratch_shapes=[pltpu.VMEM((tm, d), _BF)],
        compiler_params=_params("parallel", "arbitrary"),
        name=name,
    )(x, g.reshape(1, d), w, w)


def _matmul_resid_kernel(a_ref, w_ref, r_ref, o_ref):
    o_ref[...] = r_ref[...] + _dot(a_ref[...], w_ref[...])


def _matmul_resid(a, w, resid, *, layer, tm, tn, name):
    n, k = a.shape
    m = w.shape[2]
    return pl.pallas_call(
        _matmul_resid_kernel,
        grid=(n // tm, m // tn),
        in_specs=[
            pl.BlockSpec((tm, k), lambda i, j: (i, 0)),
            pl.BlockSpec((None, k, tn), lambda i, j: (layer, 0, j)),
            pl.BlockSpec((tm, tn), lambda i, j: (i, j)),
        ],
        out_specs=pl.BlockSpec((tm, tn), lambda i, j: (i, j)),
        out_shape=jax.ShapeDtypeStruct((n, m), _F32),
        compiler_params=_params("parallel", "arbitrary"),
        name=name,
    )(a, w, resid)


def _ffn_kernel(x_ref, xh_ref, g_ref, wa_ref, wb_ref, dwa_ref, dwb_ref, wd_ref, o_ref,
                xn_ref, ha0_ref, hb0_ref, ha1_ref, hb1_ref, act_ref, *, tm, nj, total, tiles_per_seq, act_rows):
    s = pl.program_id(0)

    @pl.when(s == 0)
    def _():
        ha1_ref[...] = jnp.zeros_like(ha1_ref)
        hb1_ref[...] = jnp.zeros_like(hb1_ref)
        o_ref[...] = jnp.zeros_like(o_ref)

    @pl.when(jnp.logical_and(s % nj == 0, s < total))
    def _():
        g = g_ref[...]
        halo = _rms_rows(xh_ref[...], g)
        halo = jnp.where((s // nj) % tiles_per_seq == 0, 0.0, halo)
        xn_ref[0:FFN_HALO, :] = halo.astype(_BF)
        xn_ref[FFN_HALO:, :] = _rms_rows(x_ref[...], g).astype(_BF)

    @pl.when((s - 1) % nj == 0)
    def _():
        o_ref[...] = x_ref[...]

    def conv(h_ref, dw, r0, rows):
        base = r0 + FFN_HALO - (FFN_CONV_WIDTH - 1)
        out = dw[0:1, :] * h_ref[base:base + rows, :]
        for k in range(1, FFN_CONV_WIDTH):
            out = out + dw[k:k + 1, :] * h_ref[base + k:base + k + rows, :]
        return out

    def step(ha_prev, hb_prev, ha_next, hb_next):
        ha_next[...] = _dot(xn_ref[...], wa_ref[...])
        hb_next[...] = _dot(xn_ref[...], wb_ref[...])
        dwa = dwa_ref[...]
        dwb = dwb_ref[...]
        for r0 in range(0, tm, act_rows):
            a = conv(ha_prev, dwa, r0, act_rows)
            b = conv(hb_prev, dwb, r0, act_rows)
            act_ref[r0:r0 + act_rows, :] = (a * jax.nn.sigmoid(a) * b).astype(_BF)
        o_ref[...] += _dot(act_ref[...], wd_ref[...])

    @pl.when(s % 2 == 0)
    def _():
        step(ha1_ref, hb1_ref, ha0_ref, hb0_ref)

    @pl.when(s % 2 == 1)
    def _():
        step(ha0_ref, hb0_ref, ha1_ref, hb1_ref)


def _ffn(x, g, w_up, dw, w_down, *, layer, seq, tm, tn, name):
    n, d = x.shape
    f = w_up.shape[2] // 2
    nj = f // tn
    total = (n // tm) * nj
    hblk = tm // FFN_HALO
    kern = functools.partial(_ffn_kernel, tm=tm, nj=nj, total=total, tiles_per_seq=seq // tm, act_rows=64)

    def cur(s):
        c = jnp.minimum(s, total - 1)
        return c // nj, c % nj

    def prev(s):
        p = jnp.maximum(s - 1, 0)
        return p // nj, p % nj

    h_scratch = pltpu.VMEM((tm + FFN_HALO, tn), _F32)
    return pl.pallas_call(
        kern,
        grid=(total + 1,),
        in_specs=[
            pl.BlockSpec((tm, d), lambda s: (cur(s)[0], 0)),
            pl.BlockSpec((FFN_HALO, d), lambda s: (jnp.maximum(cur(s)[0] * hblk - 1, 0), 0)),
            pl.BlockSpec((1, d), lambda s: (0, 0)),
            pl.BlockSpec((None, d, tn), lambda s: (layer, 0, cur(s)[1])),
            pl.BlockSpec((None, d, tn), lambda s: (layer, 0, cur(s)[1] + nj)),
            pl.BlockSpec((FFN_CONV_WIDTH, tn), lambda s: (0, prev(s)[1])),
            pl.BlockSpec((FFN_CONV_WIDTH, tn), lambda s: (0, prev(s)[1] + nj)),
            pl.BlockSpec((None, tn, d), lambda s: (layer, prev(s)[1], 0)),
        ],
        out_specs=pl.BlockSpec((tm, d), lambda s: (prev(s)[0], 0)),
        out_shape=jax.ShapeDtypeStruct((n, d), _F32),
        scratch_shapes=[pltpu.VMEM((tm + FFN_HALO, d), _BF), h_scratch, h_scratch, h_scratch, h_scratch,
                        pltpu.VMEM((tm, tn), _BF)],
        compiler_params=_params("arbitrary"),
        name=name,
    )(x, x, g.reshape(1, d), w_up, w_up, dw, dw, w_down)


def _conv_mid_kernel(u_ref, uh_ref, dw_ref, db_ref, lg_ref, lb_ref, o_ref, ext_ref, c_ref, sh_ref, *, tm, tiles_per_seq, rb, cb):
    i = pl.program_id(0)
    d = u_ref.shape[1]
    ext_ref[0:CONV_HALO, :] = jnp.where(i % tiles_per_seq == 0, 0.0, uh_ref[...])
    ext_ref[CONV_HALO:, :] = u_ref[...]
    base = CONV_HALO - (CONV_WIDTH - 1)

    def col_block(c, carry):
        cs = pl.multiple_of(c * cb, cb)
        for r in range(tm // rb):
            acc = jnp.zeros((rb, cb), _F32) + db_ref[:, pl.ds(cs, cb)]
            for phase in range(SUBLANES):
                taps = [k for k in range(CONV_WIDTH) if (base + k) % SUBLANES == phase]
                span = rb + max(base + k - phase for k in taps)
                sh_ref[0:span, :] = ext_ref[r * rb + phase:r * rb + phase + span, pl.ds(cs, cb)]
                for k in taps:
                    off = base + k - phase
                    acc = acc + sh_ref[off:off + rb, :] * dw_ref[k:k + 1, pl.ds(cs, cb)]
            c_ref[r * rb:(r + 1) * rb, pl.ds(cs, cb)] = acc
        return carry

    lax.fori_loop(0, d // cb, col_block, 0)
    c = c_ref[...]
    mu = jnp.mean(c, axis=-1, keepdims=True)
    xc = c - mu
    var = jnp.mean(xc * xc, axis=-1, keepdims=True)
    y = xc * lax.rsqrt(var + EPS) * lg_ref[...] + lb_ref[...]
    o_ref[...] = (y * jax.nn.sigmoid(y)).astype(o_ref.dtype)


def _conv_mid(u, dw, db, lg, lb, *, seq, tm, name):
    n, d = u.shape
    hblk = tm // CONV_HALO
    rb, cb = 64, 256
    kern = functools.partial(_conv_mid_kernel, tm=tm, tiles_per_seq=seq // tm, rb=rb, cb=cb)
    return pl.pallas_call(
        kern,
        grid=(n // tm,),
        in_specs=[
            pl.BlockSpec((tm, d), lambda i: (i, 0)),
            pl.BlockSpec((CONV_HALO, d), lambda i: (jnp.maximum(i * hblk - 1, 0), 0)),
            pl.BlockSpec((CONV_WIDTH, d), lambda i: (0, 0)),
            pl.BlockSpec((1, d), lambda i: (0, 0)),
            pl.BlockSpec((1, d), lambda i: (0, 0)),
            pl.BlockSpec((1, d), lambda i: (0, 0)),
        ],
        out_specs=pl.BlockSpec((tm, d), lambda i: (i, 0)),
        out_shape=jax.ShapeDtypeStruct((n, d), _BF),
        scratch_shapes=[pltpu.VMEM((tm + CONV_HALO, d), _F32), pltpu.VMEM((tm, d), _F32),
                        pltpu.VMEM((rb + CONV_HALO, cb), _F32)],
        compiler_params=_params("parallel"),
        name=name,
    )(u, u, dw, db.reshape(1, d), lg.reshape(1, d), lb.reshape(1, d))


def _gates_kernel(x_ref, g_ref, whi_ref, wlo_ref, b_ref, o_ref):
    xn = _rms_rows(x_ref[...], g_ref[...])
    xhi = xn.astype(_BF)
    xlo = (xn - xhi.astype(_F32)).astype(_BF)
    whi = whi_ref[...]
    z = _dot(xhi, whi) + _dot(xlo, whi) + _dot(xhi, wlo_ref[...]) + b_ref[...]
    log_f = jnp.minimum(z, 0.0) - jnp.log(1.0 + jnp.exp(-jnp.abs(z)))
    col = lax.broadcasted_iota(jnp.int32, z.shape, 1)
    o_ref[...] = jnp.where(col < M_HEADS, z, log_f)


def _gates(x, g, w_gate, bias, *, tm, name):
    n, d = x.shape
    ng = w_gate.shape[1]
    wp = jnp.zeros((d, LANES), _F32).at[:, :ng].set(w_gate)
    whi = wp.astype(_BF)
    wlo = (wp - whi.astype(_F32)).astype(_BF)
    bp = jnp.zeros((1, LANES), _F32).at[0, :ng].set(bias)
    out = pl.pallas_call(
        _gates_kernel,
        grid=(n // tm,),
        in_specs=[
            pl.BlockSpec((tm, d), lambda i: (i, 0)),
            pl.BlockSpec((1, d), lambda i: (0, 0)),
            pl.BlockSpec((d, LANES), lambda i: (0, 0)),
            pl.BlockSpec((d, LANES), lambda i: (0, 0)),
            pl.BlockSpec((1, LANES), lambda i: (0, 0)),
        ],
        out_specs=pl.BlockSpec((tm, LANES), lambda i: (i, 0)),
        out_shape=jax.ShapeDtypeStruct((n, LANES), _F32),
        compiler_params=_params("parallel"),
        name=name,
    )(x, g.reshape(1, d), whi, wlo, bp)
    return out[:, :ng]


def _mlstm_kernel(q_ref, k_ref, v_ref, o_ref, gc_ref, gr_ref, hg_ref, out_ref, c_ref, n_ref, m_ref, *, chunk):
    ln = chunk
    scale = M_DQK ** -0.5

    @pl.when(pl.program_id(1) == 0)
    def _():
        c_ref[...] = jnp.zeros_like(c_ref)
        n_ref[...] = jnp.zeros_like(n_ref)
        m_ref[...] = jnp.zeros_like(m_ref)

    gc = gc_ref[0]
    gr = gr_ref[0]
    rows = lax.broadcasted_iota(jnp.int32, (ln, ln), 0)
    cols = lax.broadcasted_iota(jnp.int32, (ln, ln), 1)
    causal = rows >= cols
    tril = causal.astype(_BF)
    triu = (rows <= cols).astype(_BF)
    cum_col = sum(_dot(tril, p) for p in _split3(gc))
    cum_row = sum(_dot(p, triu) for p in _split3(gr))

    for h in range(M_HEADS):
        f = M_HEADS + h
        bcol = cum_col[:, f:f + 1]
        brow = cum_row[f:f + 1, :]
        icol = gc[:, h:h + 1]
        irow = gr[h:h + 1, :]
        m_prev = m_ref[h, 0:1, 0:1]
        nrow = n_ref[h, 0:1, :]
        qh = q_ref[0, :, h * M_DQK:(h + 1) * M_DQK]
        kh = k_ref[0, :, h * M_DQK:(h + 1) * M_DQK]
        vb = v_ref[0, :, h * M_DV:(h + 1) * M_DV].astype(_BF)
        qb = qh.astype(_BF)

        dmat = jnp.where(causal, bcol - brow + irow, -jnp.inf)
        inter = bcol + m_prev
        m_t = jnp.maximum(inter, jnp.max(dmat, axis=1, keepdims=True))
        w_intra = jnp.exp(dmat - m_t)
        w_inter = jnp.exp(inter - m_t)
        s = lax.dot_general(qb, kh.astype(_BF), (((1,), (1,)), ((), ())), preferred_element_type=_F32)
        s = s * (w_intra * scale)
        qc = _dot(qb, c_ref[h].astype(_BF)) * scale
        num = w_inter * qc + _dot(s.astype(_BF), vb)
        qn = jnp.sum(qh * nrow, axis=1, keepdims=True) * scale
        den = w_inter * qn + jnp.sum(s, axis=1, keepdims=True)
        hh = num / jnp.maximum(jnp.abs(den), jnp.exp(-m_t))

        bl = bcol[ln - 1:ln, :]
        g_col = bl - bcol + icol
        g_row = bl - brow + irow
        m_new = jnp.maximum(bl + m_prev, jnp.max(g_row, axis=1, keepdims=True))
        a_prev = jnp.exp(bl + m_prev - m_new)
        kw = kh * jnp.exp(g_col - m_new)
        upd = lax.dot_general(kw.astype(_BF), vb, (((0,), (0,)), ((), ())), preferred_element_type=_F32)
        c_ref[h] = a_prev * c_ref[h] + upd
        n_new = a_prev * nrow + jnp.sum(kw, axis=0, keepdims=True)
        n_ref[h] = jnp.broadcast_to(n_new, (SUBLANES, M_DQK))
        m_ref[h] = jnp.broadcast_to(m_new, (SUBLANES, LANES))

        hn = hh * lax.rsqrt(jnp.mean(hh * hh, axis=1, keepdims=True) + EPS) * hg_ref[:, h * M_DV:(h + 1) * M_DV]
        og = o_ref[0, :, h * M_DV:(h + 1) * M_DV]
        out_ref[0, :, h * M_DV:(h + 1) * M_DV] = (hn * jax.nn.sigmoid(og)).astype(out_ref.dtype)


def _mlstm_core(proj, gates_col, gates_row, head_g, *, name):
    b, t, _ = proj.shape
    hk = M_HEADS * M_DQK
    hv = M_HEADS * M_DV
    ln = MLSTM_CHUNK
    ng = 2 * M_HEADS
    kern = functools.partial(_mlstm_kernel, chunk=ln)
    return pl.pallas_call(
        kern,
        grid=(b, t // ln),
        in_specs=[
            pl.BlockSpec((1, ln, hk), lambda i, c: (i, c, 0)),
            pl.BlockSpec((1, ln, hk), lambda i, c: (i, c, 1)),
            pl.BlockSpec((1, ln, hv), lambda i, c: (i, c, (2 * hk) // hv)),
            pl.BlockSpec((1, ln, hv), lambda i, c: (i, c, (2 * hk) // hv + 1)),
            pl.BlockSpec((1, ln, ng), lambda i, c: (i, c, 0)),
            pl.BlockSpec((1, ng, ln), lambda i, c: (i, 0, c)),
            pl.BlockSpec((1, hv), lambda i, c: (0, 0)),
        ],
        out_specs=pl.BlockSpec((1, ln, hv), lambda i, c: (i, c, 0)),
        out_shape=jax.ShapeDtypeStruct((b, t, hv), _BF),
        scratch_shapes=[
            pltpu.VMEM((M_HEADS, M_DQK, M_DV), _F32),
            pltpu.VMEM((M_HEADS, SUBLANES, M_DQK), _F32),
            pltpu.VMEM((M_HEADS, SUBLANES, LANES), _F32),
        ],
        compiler_params=_params("parallel", "arbitrary"),
        name=name,
    )(proj, proj, proj, proj, gates_col, gates_row, head_g.reshape(1, hv))


def _dsa_prep_kernel(p_ref, qg_ref, kg_ref, qt_ref, qit_ref, k_ref, ki_ref, vt_ref, *, offs):
    o_k, o_v, o_qi, o_ki = offs
    qg = qg_ref[...]
    scale = A_DHEAD ** -0.5 * LOG2_E
    for h in range(A_HEADS):
        sl = slice(h * A_DHEAD, (h + 1) * A_DHEAD)
        qh = _rms_rows(p_ref[:, sl], qg) * scale
        qt_ref[0, 0, :, h * Q_BLOCK:(h + 1) * Q_BLOCK] = qh.T.astype(_BF)
    pair = LANES // IDX_DIM
    for g in range(IDX_HEADS // pair):
        slab = p_ref[:, o_qi + g * LANES:o_qi + (g + 1) * LANES].T
        for e in range(pair):
            h = g * pair + e
            qit_ref[0, 0, :, h * Q_BLOCK:(h + 1) * Q_BLOCK] = slab[e * IDX_DIM:(e + 1) * IDX_DIM, :].astype(_BF)
    k_ref[...] = _rms_rows(p_ref[:, o_k:o_v], kg_ref[...]).astype(_BF)
    ki_ref[...] = p_ref[:, o_ki:o_ki + IDX_DIM].astype(_BF)
    vt_ref[0, 0:A_DHEAD, :] = p_ref[:, o_v:o_qi].T.astype(_BF)
    ones_row = lax.broadcasted_iota(jnp.int32, (SUBLANES, Q_BLOCK), 0) == 0
    vt_ref[0, A_DHEAD:, :] = ones_row.astype(_F32).astype(_BF)


def _dsa_prep(proj, q_g, k_g, *, batch, seq, offs, name):
    n, width = proj.shape
    nb = seq // Q_BLOCK
    hq = A_HEADS * Q_BLOCK
    kern = functools.partial(_dsa_prep_kernel, offs=offs)
    return pl.pallas_call(
        kern,
        grid=(batch, nb),
        in_specs=[
            pl.BlockSpec((Q_BLOCK, width), lambda i, j: (i * nb + j, 0)),
            pl.BlockSpec((1, A_DHEAD), lambda i, j: (0, 0)),
            pl.BlockSpec((1, A_DHEAD), lambda i, j: (0, 0)),
        ],
        out_specs=[
            pl.BlockSpec((1, 1, A_DHEAD, hq), lambda i, j: (i, j, 0, 0)),
            pl.BlockSpec((1, 1, IDX_DIM, hq), lambda i, j: (i, j, 0, 0)),
            pl.BlockSpec((Q_BLOCK, A_DHEAD), lambda i, j: (i * nb + j, 0)),
            pl.BlockSpec((Q_BLOCK, IDX_DIM), lambda i, j: (i * nb + j, 0)),
            pl.BlockSpec((1, A_DHEAD + SUBLANES, Q_BLOCK), lambda i, j: (i, 0, j)),
        ],
        out_shape=[
            jax.ShapeDtypeStruct((batch, nb, A_DHEAD, hq), _BF),
            jax.ShapeDtypeStruct((batch, nb, IDX_DIM, hq), _BF),
            jax.ShapeDtypeStruct((n, A_DHEAD), _BF),
            jax.ShapeDtypeStruct((n, IDX_DIM), _BF),
            jax.ShapeDtypeStruct((batch, A_DHEAD + SUBLANES, seq), _BF),
        ],
        compiler_params=_params("parallel", "parallel"),
        name=name,
    )(proj, q_g.reshape(1, A_DHEAD), k_g.reshape(1, A_DHEAD))


def _dsa_kernel(bound_ref, ki_ref, k_ref, vt_ref, qit_ref, wi_ref, qt_ref, o_ref,
                key_ref, hi_ref, lo_ref, stat_ref, cut_ref, acc_ref, m_ref, alpha_ref, p_ref, *, tk, topk):
    q0 = pl.program_id(1) * Q_BLOCK
    n_kt = (q0 + Q_BLOCK - 1) // tk + 1
    n_bits_idx = (key_ref.shape[0] - 1).bit_length()
    lane_q = q0 + lax.broadcasted_iota(jnp.int32, (tk, Q_BLOCK), 1)
    row = lax.broadcasted_iota(jnp.int32, (tk, Q_BLOCK), 0)

    qit = qit_ref[0, 0]
    wi = wi_ref[0, 0] * ((IDX_HEADS * IDX_DIM) ** -0.5)

    def score_tile(kt, carry):
        start = pl.multiple_of(kt * tk, tk)
        lg = _dot(ki_ref[0, pl.ds(start, tk), :], qit)
        w = jnp.maximum(lg, 0.0) * wi
        s = w[:, 0:Q_BLOCK]
        for h in range(1, IDX_HEADS):
            s = s + w[:, h * Q_BLOCK:(h + 1) * Q_BLOCK]
        bits = pltpu.bitcast(s, jnp.int32)
        key = bits ^ ((bits >> 31) & 0x7FFFFFFF)
        key = jnp.where(bits == INT_MIN, 0, key)
        key = jnp.where(start + row <= lane_q, key, INT_MIN)
        key_ref[pl.ds(start, tk), :] = key
        return carry

    lax.fori_loop(0, n_kt, score_tile, 0)

    ck = 2 * tk
    n_ct = (n_kt + 1) // 2

    @pl.when(n_kt % 2 == 1)
    def _():
        key_ref[pl.ds(pl.multiple_of(n_kt * tk, tk), tk), :] = jnp.full((tk, Q_BLOCK), INT_MIN, jnp.int32)

    crow = lax.broadcasted_iota(jnp.int32, (ck, Q_BLOCK), 0)

    def count(pred):
        def body(ct, acc):
            start = pl.multiple_of(ct * ck, ck)
            hit = pred(key_ref[pl.ds(start, ck), :], start + crow).astype(jnp.int32)
            return acc + jnp.sum(hit.reshape(ck // SUBLANES, SUBLANES, Q_BLOCK), axis=0)
        acc = lax.fori_loop(0, n_ct, body, jnp.zeros((SUBLANES, Q_BLOCK), jnp.int32))
        return jnp.sum(acc, axis=0, keepdims=True)

    half = 1 << 15

    def split_tile(ct, carry):
        start = pl.multiple_of(ct * ck, ck)
        keys = key_ref[pl.ds(start, ck), :]
        hi_ref[pl.ds(start, ck), :] = (keys >> 16).astype(jnp.int16)
        lo_ref[pl.ds(start, ck), :] = ((keys & 0xFFFF) - half).astype(jnp.int16)
        return carry

    lax.fori_loop(0, n_ct, split_tile, 0)

    span = 2 * ck
    n_span = (n_kt * tk + span - 1) // span

    @pl.when(n_ct * ck < n_span * span)
    def _():
        start = pl.multiple_of(n_ct * ck, ck)
        hi_ref[pl.ds(start, ck), :] = jnp.full((ck, Q_BLOCK), -half, jnp.int16)
        lo_ref[pl.ds(start, ck), :] = jnp.full((ck, Q_BLOCK), -half, jnp.int16)

    def threshold_stats(rows):
        def count16(ref, pred):
            total = None
            for r0 in range(0, rows, ck):
                hit = pred(ref[r0:r0 + ck, :]).astype(jnp.int16)
                parts = [hit[t0:t0 + 2 * SUBLANES, :] for t0 in range(0, ck, 2 * SUBLANES)]
                while len(parts) > 1:
                    parts = [a + b for a, b in zip(parts[0::2], parts[1::2])]
                total = parts[0] if total is None else total + parts[0]
            return jnp.sum(total.astype(jnp.int32), axis=0, keepdims=True)

        def search16(ref, base_count):
            def bit(b, v):
                cand = v | lax.shift_left(jnp.int32(1), 15 - b)
                cand16 = (cand - half).astype(jnp.int16)
                cnt = base_count + count16(ref, lambda blk: blk >= cand16)
                return jnp.where(cnt >= topk, cand, v)
            return lax.fori_loop(0, 16, bit, jnp.zeros((1, Q_BLOCK), jnp.int32))

        thr_hi = search16(hi_ref, 0) - half
        thr_hi16 = thr_hi.astype(jnp.int16)
        next_hi16 = (thr_hi + 1).astype(jnp.int16)
        above = count16(hi_ref, lambda blk: blk >= next_hi16)
        above = jnp.where(thr_hi == half - 1, 0, above)
        for r0 in range(0, rows, ck):
            same = hi_ref[r0:r0 + ck, :] == thr_hi16
            lo_ref[r0:r0 + ck, :] = jnp.where(same, lo_ref[r0:r0 + ck, :], jnp.int16(-half))
        thr_lo = search16(lo_ref, above)
        thr_lo16 = (thr_lo - half).astype(jnp.int16)
        n_above = above + count16(lo_ref, lambda blk: blk > thr_lo16)
        n_same = count16(lo_ref, lambda blk: blk == thr_lo16)
        stat_ref[0:1, :] = (thr_hi << 16) | thr_lo
        stat_ref[1:2, :] = n_above
        stat_ref[2:3, :] = n_same

    for c in range(1, key_ref.shape[0] // span + 1):
        pl.when(n_span == c)(functools.partial(threshold_stats, c * span))

    thr = stat_ref[0:1, :]
    n_gt = stat_ref[1:2, :]
    n_eq = stat_ref[2:3, :]
    need = topk - n_gt
    cut_ref[...] = jnp.full((1, Q_BLOCK), 2 ** 30, jnp.int32)
    tied = jnp.logical_and(n_eq != need, thr != INT_MIN)

    @pl.when(jnp.max(tied.astype(jnp.int32)) > 0)
    def _():
        def cut_bit(b, p):
            cand = p | lax.shift_left(jnp.int32(1), n_bits_idx - 1 - b)
            cnt = count(lambda keys, idx: jnp.logical_and(keys == thr, idx < cand))
            return jnp.where(cnt < need, cand, p)
        cut_ref[...] = lax.fori_loop(0, n_bits_idx, cut_bit, jnp.zeros((1, Q_BLOCK), jnp.int32))

    cut = cut_ref[...]

    acc_ref[...] = jnp.zeros(acc_ref.shape, _F32)
    qt = qt_ref[0, 0]
    bound = bound_ref[0, 0]
    fixed_shift = 2.0 * bound <= MAX_FIXED_SHIFT_SPAN

    def selected(kt):
        start = pl.multiple_of(kt * tk, tk)
        keys = key_ref[pl.ds(start, tk), :]
        sel = jnp.logical_or(keys > thr, jnp.logical_and(keys == thr, start + row <= cut))
        return start, jnp.logical_and(sel, keys != INT_MIN)

    @pl.when(fixed_shift)
    def _():
        def att_tile(kt, carry):
            start, sel = selected(kt)
            mk = jnp.where(sel, -bound, NEG_BIG)
            lg = _dot(k_ref[0, pl.ds(start, tk), :], qt)
            for h in range(A_HEADS):
                sl = slice(h * Q_BLOCK, (h + 1) * Q_BLOCK)
                p_ref[:, sl] = jnp.exp2(lg[:, sl] + mk).astype(_BF)
            acc_ref[...] += _dot(vt_ref[0, :, pl.ds(start, tk)], p_ref[...])
            return carry

        lax.fori_loop(0, n_kt, att_tile, 0)

    @pl.when(jnp.logical_not(fixed_shift))
    def _():
        m_ref[...] = jnp.full(m_ref.shape, NEG_BIG, _F32)

        def att_tile(kt, carry):
            start, sel = selected(kt)
            mk = jnp.where(sel, 0.0, NEG_BIG)
            lg = _dot(k_ref[0, pl.ds(start, tk), :], qt)
            for h in range(A_HEADS):
                sl = slice(h * Q_BLOCK, (h + 1) * Q_BLOCK)
                x = lg[:, sl] + mk
                m_old = m_ref[:, sl]
                m_new = jnp.maximum(m_old, jnp.max(x, axis=0, keepdims=True))
                p_ref[:, sl] = jnp.exp2(x - m_new).astype(_BF)
                alpha_ref[:, sl] = jnp.exp2(m_old - m_new)
                m_ref[:, sl] = m_new
            pv = _dot(vt_ref[0, :, pl.ds(start, tk)], p_ref[...])
            acc_ref[...] = acc_ref[...] * alpha_ref[...] + pv
            return carry

        lax.fori_loop(0, n_kt, att_tile, 0)

    for h in range(A_HEADS):
        sl = slice(h * Q_BLOCK, (h + 1) * Q_BLOCK)
        out_h = acc_ref[0:A_DHEAD, sl] / acc_ref[A_DHEAD:A_DHEAD + 1, sl]
        o_ref[0, :, h * A_DHEAD:(h + 1) * A_DHEAD] = out_h.T.astype(o_ref.dtype)


def _dsa_attend(bound, ki, k, vt, qit, wi, qt, *, name):
    b, t, _ = ki.shape
    nb = t // Q_BLOCK
    hq = A_HEADS * Q_BLOCK
    topk = min(TOPK_MAX, t // 4)
    kern = functools.partial(_dsa_kernel, tk=KEY_TILE, topk=topk)
    return pl.pallas_call(
        kern,
        grid=(b, nb),
        in_specs=[
            pl.BlockSpec(memory_space=pltpu.SMEM),
            pl.BlockSpec((1, t, IDX_DIM), lambda i, j: (i, 0, 0)),
            pl.BlockSpec((1, t, A_DHEAD), lambda i, j: (i, 0, 0)),
            pl.BlockSpec((1, vt.shape[1], t), lambda i, j: (i, 0, 0)),
            pl.BlockSpec((1, 1, IDX_DIM, hq), lambda i, j: (i, j, 0, 0)),
            pl.BlockSpec((1, 1, 1, hq), lambda i, j: (i, j, 0, 0)),
            pl.BlockSpec((1, 1, A_DHEAD, hq), lambda i, j: (i, j, 0, 0)),
        ],
        out_specs=pl.BlockSpec((1, Q_BLOCK, A_HEADS * A_DHEAD), lambda i, j: (i, j, 0)),
        out_shape=jax.ShapeDtypeStruct((b, t, A_HEADS * A_DHEAD), _BF),
        scratch_shapes=[
            pltpu.VMEM((t, Q_BLOCK), jnp.int32),
            pltpu.VMEM((t, Q_BLOCK), jnp.int16),
            pltpu.VMEM((t, Q_BLOCK), jnp.int16),
            pltpu.VMEM((SUBLANES, Q_BLOCK), jnp.int32),
            pltpu.VMEM((1, Q_BLOCK), jnp.int32),
            pltpu.VMEM((vt.shape[1], hq), _F32),
            pltpu.VMEM((1, hq), _F32),
            pltpu.VMEM((1, hq), _F32),
            pltpu.VMEM((KEY_TILE, hq), _BF),
        ],
        compiler_params=_params("parallel", "arbitrary"),
        name=name,
    )(bound.reshape(1, 1), ki, k, vt, qit, wi, qt)


def _mlstm_layer(x2, norm_g, w_in_bf, w_gate, gate_b, head_g, w_out_bf, *, layer, batch, seq):
    n, d = x2.shape
    hk = M_HEADS * M_DQK
    hv = M_HEADS * M_DV
    main = 2 * hk + 2 * hv
    proj = _norm_matmul(x2, norm_g, w_in_bf, layer=layer, cols=main, tm=1024, tn=1024, name="mlstm_in")
    gates = _gates(x2, norm_g, w_gate, gate_b, tm=512, name="mlstm_gates")
    gates = gates.reshape(batch, seq, 2 * M_HEADS)
    h = _mlstm_core(proj.reshape(batch, seq, main), gates, gates.transpose(0, 2, 1), head_g, name="mlstm_core")
    return _matmul_resid(h.reshape(n, hv), w_out_bf, x2, layer=layer, tm=512, tn=d, name="mlstm_out")


def _dsa_layer(x2, norm_g, w_in, q_g, k_g, w_out, *, batch, seq):
    n, d = x2.shape
    qd = A_HEADS * A_DHEAD
    qid = IDX_HEADS * IDX_DIM
    o_k, o_v, o_qi = qd, qd + A_DHEAD, qd + 2 * A_DHEAD
    o_ki, o_wi, o_end = o_qi + qid, o_qi + qid + IDX_DIM, o_qi + qid + IDX_DIM + IDX_HEADS
    tn = 512
    padded = -(-o_end // tn) * tn
    w_pad = jnp.zeros((1, d, padded), _BF).at[0, :, :o_end].set(w_in.astype(_BF))
    proj = _norm_matmul(x2, norm_g, w_pad, layer=0, cols=padded, tm=1024, tn=tn, name="dsa_in")
    qt, qit, kn, ki, vt = _dsa_prep(proj, q_g, k_g, batch=batch, seq=seq, offs=(o_k, o_v, o_qi, o_ki), name="dsa_qknorm")
    nb = seq // Q_BLOCK
    wi = proj[:, o_wi:o_end].reshape(batch, nb, Q_BLOCK, IDX_HEADS).transpose(0, 1, 3, 2).reshape(batch, nb, 1, IDX_HEADS * Q_BLOCK)
    bound = 1.01 * LOG2_E * A_DHEAD ** 0.5 * jnp.max(jnp.abs(q_g)) * jnp.max(jnp.abs(k_g))
    out = _dsa_attend(bound.astype(_F32), ki.reshape(batch, seq, IDX_DIM), kn.reshape(batch, seq, A_DHEAD), vt, qit, wi, qt, name="dsa_attend")
    return _matmul_resid(out.reshape(n, qd), w_out.astype(_BF)[None], x2, layer=0, tm=512, tn=d, name="dsa_out")


def _conv_layer(x2, norm_g, w_in, dw_w, dw_b, ln_g, ln_b, w_out, *, seq):
    u = _norm_glu(x2, norm_g, w_in.astype(_BF), tm=1024, tn=512, name="conv_in")
    v = _conv_mid(u, dw_w, dw_b, ln_g, ln_b, seq=seq, tm=256, name="conv_mid")
    return _matmul_resid(v, w_out.astype(_BF)[None], x2, layer=0, tm=512, tn=x2.shape[1], name="conv_out")


def kernel(x, mix_norm_g, ffn_norm_g, mlstm_w_in, mlstm_gate_b, mlstm_head_g, mlstm_w_out,
           dsa_w_in, dsa_q_g, dsa_k_g, dsa_w_out, conv_w_in, conv_dw_w, conv_dw_b,
           conv_ln_g, conv_ln_b, conv_w_out, ffn_w_up, ffn_dw_w, ffn_w_down):
    batch, seq, d = x.shape
    depth = mix_norm_g.shape[0]
    x2 = x.reshape(batch * seq, d)
    main = 2 * M_HEADS * (M_DQK + M_DV)
    mlstm_w_in_bf = mlstm_w_in.astype(_BF)
    mlstm_w_out_bf = mlstm_w_out.astype(_BF)
    ffn_w_up_bf = ffn_w_up.astype(_BF)
    ffn_w_down_bf = ffn_w_down.astype(_BF)
    for i in range(depth):
        kind = i % N_MIXERS
        j = i // N_MIXERS
        if kind == 0:
            x2 = _mlstm_layer(x2, mix_norm_g[i], mlstm_w_in_bf, mlstm_w_in[j, :, main:], mlstm_gate_b[j], mlstm_head_g[j],
                              mlstm_w_out_bf, layer=j, batch=batch, seq=seq)
        elif kind == 1:
            x2 = _dsa_layer(x2, mix_norm_g[i], dsa_w_in[j], dsa_q_g[j], dsa_k_g[j], dsa_w_out[j], batch=batch, seq=seq)
        else:
            x2 = _conv_layer(x2, mix_norm_g[i], conv_w_in[j], conv_dw_w[j], conv_dw_b[j], conv_ln_g[j], conv_ln_b[j],
                             conv_w_out[j], seq=seq)
        x2 = _ffn(x2, ffn_norm_g[i], ffn_w_up_bf, ffn_dw_w[i], ffn_w_down_bf, layer=i, seq=seq, tm=512, tn=512, name="ffn")
    return x2.reshape(batch, seq, d)
```

```python
import functools

import jax
import jax.numpy as jnp
from jax import lax
from jax.experimental import pallas as pl
from jax.experimental.pallas import tpu as pltpu

N_MIXERS = 3
EPS = 1e-6
M_HEADS = 8
M_DQK = 128
M_DV = 256
A_HEADS = 16
A_DHEAD = 128
IDX_HEADS = 16
IDX_DIM = 64
TOPK_MAX = 256
CONV_WIDTH = 31
FFN_CONV_WIDTH = 3

LANES = 128
SUBLANES = 8
FFN_HALO = 16
VMEM_LIMIT = 52 * 1024 * 1024
MLSTM_CHUNK = 256
Q_BLOCK = 128
KEY_TILE = 512
SEARCH_SPAN = 1024
CONV_HALO = 32
LOG2_E = 1.4426950408889634
INT_MIN = -(2 ** 31)
NEG_BIG = -1e30
MAX_FIXED_SHIFT_SPAN = 120.0

_BF = jnp.bfloat16
_F32 = jnp.float32


def _params(*sem):
    return pltpu.CompilerParams(dimension_semantics=sem, vmem_limit_bytes=VMEM_LIMIT)


def _rms_rows(x, g):
    ms = jnp.mean(x * x, axis=-1, keepdims=True)
    return x * lax.rsqrt(ms + EPS) * g


def _dot(a, b):
    return jnp.dot(a, b, preferred_element_type=_F32)


def _split3(x):
    hi = x.astype(_BF)
    r1 = x - hi.astype(_F32)
    mid = r1.astype(_BF)
    lo = (r1 - mid.astype(_F32)).astype(_BF)
    return hi, mid, lo


def _norm_matmul_kernel(x_ref, g_ref, w_ref, o_ref, xn_ref):
    @pl.when(pl.program_id(1) == 0)
    def _():
        xn_ref[...] = _rms_rows(x_ref[...], g_ref[...]).astype(_BF)

    o_ref[...] = _dot(xn_ref[...], w_ref[...]).astype(o_ref.dtype)


def _norm_matmul(x, g, w, *, layer, cols, tm, tn, name):
    n, d = x.shape
    m = cols
    return pl.pallas_call(
        _norm_matmul_kernel,
        grid=(n // tm, m // tn),
        in_specs=[
            pl.BlockSpec((tm, d), lambda i, j: (i, 0)),
            pl.BlockSpec((1, d), lambda i, j: (0, 0)),
            pl.BlockSpec((None, d, tn), lambda i, j: (layer, 0, j)),
        ],
        out_specs=pl.BlockSpec((tm, tn), lambda i, j: (i, j)),
        out_shape=jax.ShapeDtypeStruct((n, m), _F32),
        scratch_shapes=[pltpu.VMEM((tm, d), _BF)],
        compiler_params=_params("parallel", "arbitrary"),
        name=name,
    )(x, g.reshape(1, d), w)


def _norm_glu_kernel(x_ref, g_ref, wa_ref, wg_ref, o_ref, xn_ref):
    @pl.when(pl.program_id(1) == 0)
    def _():
        xn_ref[...] = _rms_rows(x_ref[...], g_ref[...]).astype(_BF)

    xn = xn_ref[...]
    a = _dot(xn, wa_ref[...])
    gate = _dot(xn, wg_ref[...])
    o_ref[...] = a * jax.nn.sigmoid(gate)


def _norm_glu(x, g, w, *, tm, tn, name):
    n, d = x.shape
    half = w.shape[1] // 2
    nj = half // tn
    return pl.pallas_call(
        _norm_glu_kernel,
        grid=(n // tm, nj),
        in_specs=[
            pl.BlockSpec((tm, d), lambda i, j: (i, 0)),
            pl.BlockSpec((1, d), lambda i, j: (0, 0)),
            pl.BlockSpec((d, tn), lambda i, j: (0, j)),
            pl.BlockSpec((d, tn), lambda i, j: (0, j + nj)),
        ],
        out_specs=pl.BlockSpec((tm, tn), lambda i, j: (i, j)),
        out_shape=jax.ShapeDtypeStruct((n, half), _F32),
        scratch_shapes=[pltpu.VMEM((tm, d), _BF)],
        compiler_params=_params("parallel", "arbitrary"),
        name=name,
    )(x, g.reshape(1, d), w, w)


def _matmul_resid_kernel(a_ref, w_ref, r_ref, o_ref):
    o_ref[...] = r_ref[...] + _dot(a_ref[...], w_ref[...])


def _matmul_resid(a, w, resid, *, layer, tm, tn, name):
    n, k = a.shape
    m = w.shape[2]
    return pl.pallas_call(
        _matmul_resid_kernel,
        grid=(n // tm, m // tn),
        in_specs=[
            pl.BlockSpec((tm, k), lambda i, j: (i, 0)),
            pl.BlockSpec((None, k, tn), lambda i, j: (layer, 0, j)),
            pl.BlockSpec((tm, tn), lambda i, j: (i, j)),
        ],
        out_specs=pl.BlockSpec((tm, tn), lambda i, j: (i, j)),
        out_shape=jax.ShapeDtypeStruct((n, m), _F32),
        compiler_params=_params("parallel", "arbitrary"),
        name=name,
    )(a, w, resid)


def _ffn_kernel(x_ref, xh_ref, g_ref, wa_ref, wb_ref, dwa_ref, dwb_ref, wd_ref, o_ref,
                xn_ref, ha0_ref, hb0_ref, ha1_ref, hb1_ref, act_ref, *, tm, nj, total, tiles_per_seq, act_rows):
    s = pl.program_id(0)

    @pl.when(s == 0)
    def _():
        ha1_ref[...] = jnp.zeros_like(ha1_ref)
        hb1_ref[...] = jnp.zeros_like(hb1_ref)
        o_ref[...] = jnp.zeros_like(o_ref)

    @pl.when(jnp.logical_and(s % nj == 0, s < total))
    def _():
        g = g_ref[...]
        halo = _rms_rows(xh_ref[...], g)
        halo = jnp.where((s // nj) % tiles_per_seq == 0, 0.0, halo)
        xn_ref[0:FFN_HALO, :] = halo.astype(_BF)
        xn_ref[FFN_HALO:, :] = _rms_rows(x_ref[...], g).astype(_BF)

    @pl.when((s - 1) % nj == 0)
    def _():
        o_ref[...] = x_ref[...]

    def conv(h_ref, dw, r0, rows):
        base = r0 + FFN_HALO - (FFN_CONV_WIDTH - 1)
        out = dw[0:1, :] * h_ref[base:base + rows, :]
        for k in range(1, FFN_CONV_WIDTH):
            out = out + dw[k:k + 1, :] * h_ref[base + k:base + k + rows, :]
        return out

    def step(ha_prev, hb_prev, ha_next, hb_next):
        ha_next[...] = _dot(xn_ref[...], wa_ref[...])
        hb_next[...] = _dot(xn_ref[...], wb_ref[...])
        dwa = dwa_ref[...]
        dwb = dwb_ref[...]
        for r0 in range(0, tm, act_rows):
            a = conv(ha_prev, dwa, r0, act_rows)
            b = conv(hb_prev, dwb, r0, act_rows)
            act_ref[r0:r0 + act_rows, :] = (a * jax.nn.sigmoid(a) * b).astype(_BF)
        o_ref[...] += _dot(act_ref[...], wd_ref[...])

    @pl.when(s % 2 == 0)
    def _():
        step(ha1_ref, hb1_ref, ha0_ref, hb0_ref)

    @pl.when(s % 2 == 1)
    def _():
        step(ha0_ref, hb0_ref, ha1_ref, hb1_ref)


def _ffn(x, g, w_up, dw, w_down, *, layer, seq, tm, tn, name):
    n, d = x.shape
    f = w_up.shape[2] // 2
    nj = f // tn
    total = (n // tm) * nj
    hblk = tm // FFN_HALO
    kern = functools.partial(_ffn_kernel, tm=tm, nj=nj, total=total, tiles_per_seq=seq // tm, act_rows=64)

    def cur(s):
        c = jnp.minimum(s, total - 1)
        return c // nj, c % nj

    def prev(s):
        p = jnp.maximum(s - 1, 0)
        return p // nj, p % nj

    h_scratch = pltpu.VMEM((tm + FFN_HALO, tn), _F32)
    return pl.pallas_call(
        kern,
        grid=(total + 1,),
        in_specs=[
            pl.BlockSpec((tm, d), lambda s: (cur(s)[0], 0)),
            pl.BlockSpec((FFN_HALO, d), lambda s: (jnp.maximum(cur(s)[0] * hblk - 1, 0), 0)),
            pl.BlockSpec((1, d), lambda s: (0, 0)),
            pl.BlockSpec((None, d, tn), lambda s: (layer, 0, cur(s)[1])),
            pl.BlockSpec((None, d, tn), lambda s: (layer, 0, cur(s)[1] + nj)),
            pl.BlockSpec((FFN_CONV_WIDTH, tn), lambda s: (0, prev(s)[1])),
            pl.BlockSpec((FFN_CONV_WIDTH, tn), lambda s: (0, prev(s)[1] + nj)),
            pl.BlockSpec((None, tn, d), lambda s: (layer, prev(s)[1], 0)),
        ],
        out_specs=pl.BlockSpec((tm, d), lambda s: (prev(s)[0], 0)),
        out_shape=jax.ShapeDtypeStruct((n, d), _F32),
        scratch_shapes=[pltpu.VMEM((tm + FFN_HALO, d), _BF), h_scratch, h_scratch, h_scratch, h_scratch,
                        pltpu.VMEM((tm, tn), _BF)],
        compiler_params=_params("arbitrary"),
        name=name,
    )(x, x, g.reshape(1, d), w_up, w_up, dw, dw, w_down)


def _conv_mid_kernel(u_ref, uh_ref, dw_ref, db_ref, lg_ref, lb_ref, o_ref, ext_ref, c_ref, sh_ref, *, tm, tiles_per_seq, rb, cb):
    i = pl.program_id(0)
    d = u_ref.shape[1]
    ext_ref[0:CONV_HALO, :] = jnp.where(i % tiles_per_seq == 0, 0.0, uh_ref[...])
    ext_ref[CONV_HALO:, :] = u_ref[...]
    base = CONV_HALO - (CONV_WIDTH - 1)

    def col_block(c, carry):
        cs = pl.multiple_of(c * cb, cb)
        for r in range(tm // rb):
            acc = jnp.zeros((rb, cb), _F32) + db_ref[:, pl.ds(cs, cb)]
            for phase in range(SUBLANES):
                taps = [k for k in range(CONV_WIDTH) if (base + k) % SUBLANES == phase]
                span = rb + max(base + k - phase for k in taps)
                sh_ref[0:span, :] = ext_ref[r * rb + phase:r * rb + phase + span, pl.ds(cs, cb)]
                for k in taps:
                    off = base + k - phase
                    acc = acc + sh_ref[off:off + rb, :] * dw_ref[k:k + 1, pl.ds(cs, cb)]
            c_ref[r * rb:(r + 1) * rb, pl.ds(cs, cb)] = acc
        return carry

    lax.fori_loop(0, d // cb, col_block, 0)
    c = c_ref[...]
    mu = jnp.mean(c, axis=-1, keepdims=True)
    xc = c - mu
    var = jnp.mean(xc * xc, axis=-1, keepdims=True)
    y = xc * lax.rsqrt(var + EPS) * lg_ref[...] + lb_ref[...]
    o_ref[...] = (y * jax.nn.sigmoid(y)).astype(o_ref.dtype)


def _conv_mid(u, dw, db, lg, lb, *, seq, tm, name):
    n, d = u.shape
    hblk = tm // CONV_HALO
    rb, cb = 64, 256
    kern = functools.partial(_conv_mid_kernel, tm=tm, tiles_per_seq=seq // tm, rb=rb, cb=cb)
    return pl.pallas_call(
        kern,
        grid=(n // tm,),
        in_specs=[
            pl.BlockSpec((tm, d), lambda i: (i, 0)),
            pl.BlockSpec((CONV_HALO, d), lambda i: (jnp.maximum(i * hblk - 1, 0), 0)),
            pl.BlockSpec((CONV_WIDTH, d), lambda i: (0, 0)),
            pl.BlockSpec((1, d), lambda i: (0, 0)),
            pl.BlockSpec((1, d), lambda i: (0, 0)),
            pl.BlockSpec((1, d), lambda i: (0, 0)),
        ],
        out_specs=pl.BlockSpec((tm, d), lambda i: (i, 0)),
        out_shape=jax.ShapeDtypeStruct((n, d), _BF),
        scratch_shapes=[pltpu.VMEM((tm + CONV_HALO, d), _F32), pltpu.VMEM((tm, d), _F32),
                        pltpu.VMEM((rb + CONV_HALO, cb), _F32)],
        compiler_params=_params("parallel"),
        name=name,
    )(u, u, dw, db.reshape(1, d), lg.reshape(1, d), lb.reshape(1, d))


def _gates_kernel(x_ref, g_ref, whi_ref, wlo_ref, b_ref, o_ref):
    xn = _rms_rows(x_ref[...], g_ref[...])
    xhi = xn.astype(_BF)
    xlo = (xn - xhi.astype(_F32)).astype(_BF)
    whi = whi_ref[...]
    z = _dot(xhi, whi) + _dot(xlo, whi) + _dot(xhi, wlo_ref[...]) + b_ref[...]
    log_f = jnp.minimum(z, 0.0) - jnp.log(1.0 + jnp.exp(-jnp.abs(z)))
    col = lax.broadcasted_iota(jnp.int32, z.shape, 1)
    o_ref[...] = jnp.where(col < M_HEADS, z, log_f)


def _gates(x, g, w_gate, bias, *, tm, name):
    n, d = x.shape
    ng = w_gate.shape[1]
    wp = jnp.zeros((d, LANES), _F32).at[:, :ng].set(w_gate)
    whi = wp.astype(_BF)
    wlo = (wp - whi.astype(_F32)).astype(_BF)
    bp = jnp.zeros((1, LANES), _F32).at[0, :ng].set(bias)
    out = pl.pallas_call(
        _gates_kernel,
        grid=(n // tm,),
        in_specs=[
            pl.BlockSpec((tm, d), lambda i: (i, 0)),
            pl.BlockSpec((1, d), lambda i: (0, 0)),
            pl.BlockSpec((d, LANES), lambda i: (0, 0)),
            pl.BlockSpec((d, LANES), lambda i: (0, 0)),
            pl.BlockSpec((1, LANES), lambda i: (0, 0)),
        ],
        out_specs=pl.BlockSpec((tm, LANES), lambda i: (i, 0)),
        out_shape=jax.ShapeDtypeStruct((n, LANES), _F32),
        compiler_params=_params("parallel"),
        name=name,
    )(x, g.reshape(1, d), whi, wlo, bp)
    return out[:, :ng]


def _mlstm_kernel(q_ref, k_ref, v_ref, o_ref, gc_ref, gr_ref, hg_ref, out_ref, c_ref, n_ref, m_ref, *, chunk):
    ln = chunk
    scale = M_DQK ** -0.5

    @pl.when(pl.program_id(1) == 0)
    def _():
        c_ref[...] = jnp.zeros_like(c_ref)
        n_ref[...] = jnp.zeros_like(n_ref)
        m_ref[...] = jnp.zeros_like(m_ref)

    gc = gc_ref[0]
    gr = gr_ref[0]
    rows = lax.broadcasted_iota(jnp.int32, (ln, ln), 0)
    cols = lax.broadcasted_iota(jnp.int32, (ln, ln), 1)
    causal = rows >= cols
    tril = causal.astype(_BF)
    triu = (rows <= cols).astype(_BF)
    cum_col = sum(_dot(tril, p) for p in _split3(gc))
    cum_row = sum(_dot(p, triu) for p in _split3(gr))

    for h in range(M_HEADS):
        f = M_HEADS + h
        bcol = cum_col[:, f:f + 1]
        brow = cum_row[f:f + 1, :]
        icol = gc[:, h:h + 1]
        irow = gr[h:h + 1, :]
        m_prev = m_ref[h, 0:1, 0:1]
        nrow = n_ref[h, 0:1, :]
        qh = q_ref[0, :, h * M_DQK:(h + 1) * M_DQK]
        kh = k_ref[0, :, h * M_DQK:(h + 1) * M_DQK]
        vb = v_ref[0, :, h * M_DV:(h + 1) * M_DV].astype(_BF)
        qb = qh.astype(_BF)

        dmat = jnp.where(causal, bcol - brow + irow, -jnp.inf)
        inter = bcol + m_prev
        m_t = jnp.maximum(inter, jnp.max(dmat, axis=1, keepdims=True))
        w_intra = jnp.exp(dmat - m_t)
        w_inter = jnp.exp(inter - m_t)
        s = lax.dot_general(qb, kh.astype(_BF), (((1,), (1,)), ((), ())), preferred_element_type=_F32)
        s = s * (w_intra * scale)
        qc = _dot(qb, c_ref[h].astype(_BF)) * scale
        num = w_inter * qc + _dot(s.astype(_BF), vb)
        qn = jnp.sum(qh * nrow, axis=1, keepdims=True) * scale
        den = w_inter * qn + jnp.sum(s, axis=1, keepdims=True)
        hh = num / jnp.maximum(jnp.abs(den), jnp.exp(-m_t))

        bl = bcol[ln - 1:ln, :]
        g_col = bl - bcol + icol
        g_row = bl - brow + irow
        m_new = jnp.maximum(bl + m_prev, jnp.max(g_row, axis=1, keepdims=True))
        a_prev = jnp.exp(bl + m_prev - m_new)
        kw = kh * jnp.exp(g_col - m_new)
        upd = lax.dot_general(kw.astype(_BF), vb, (((0,), (0,)), ((), ())), preferred_element_type=_F32)
        c_ref[h] = a_prev * c_ref[h] + upd
        n_new = a_prev * nrow + jnp.sum(kw, axis=0, keepdims=True)
        n_ref[h] = jnp.broadcast_to(n_new, (SUBLANES, M_DQK))
        m_ref[h] = jnp.broadcast_to(m_new, (SUBLANES, LANES))

        hn = hh * lax.rsqrt(jnp.mean(hh * hh, axis=1, keepdims=True) + EPS) * hg_ref[:, h * M_DV:(h + 1) * M_DV]
        og = o_ref[0, :, h * M_DV:(h + 1) * M_DV]
        out_ref[0, :, h * M_DV:(h + 1) * M_DV] = (hn * jax.nn.sigmoid(og)).astype(out_ref.dtype)


def _mlstm_core(proj, gates_col, gates_row, head_g, *, name):
    b, t, _ = proj.shape
    hk = M_HEADS * M_DQK
    hv = M_HEADS * M_DV
    ln = MLSTM_CHUNK
    ng = 2 * M_HEADS
    kern = functools.partial(_mlstm_kernel, chunk=ln)
    return pl.pallas_call(
        kern,
        grid=(b, t // ln),
        in_specs=[
            pl.BlockSpec((1, ln, hk), lambda i, c: (i, c, 0)),
            pl.BlockSpec((1, ln, hk), lambda i, c: (i, c, 1)),
            pl.BlockSpec((1, ln, hv), lambda i, c: (i, c, (2 * hk) // hv)),
            pl.BlockSpec((1, ln, hv), lambda i, c: (i, c, (2 * hk) // hv + 1)),
            pl.BlockSpec((1, ln, ng), lambda i, c: (i, c, 0)),
            pl.BlockSpec((1, ng, ln), lambda i, c: (i, 0, c)),
            pl.BlockSpec((1, hv), lambda i, c: (0, 0)),
        ],
        out_specs=pl.BlockSpec((1, ln, hv), lambda i, c: (i, c, 0)),
        out_shape=jax.ShapeDtypeStruct((b, t, hv), _BF),
        scratch_shapes=[
            pltpu.VMEM((M_HEADS, M_DQK, M_DV), _F32),
            pltpu.VMEM((M_HEADS, SUBLANES, M_DQK), _F32),
            pltpu.VMEM((M_HEADS, SUBLANES, LANES), _F32),
        ],
        compiler_params=_params("parallel", "arbitrary"),
        name=name,
    )(proj, proj, proj, proj, gates_col, gates_row, head_g.reshape(1, hv))


def _dsa_prep_kernel(p_ref, qg_ref, kg_ref, qt_ref, qit_ref, k_ref, ki_ref, vt_ref, *, offs):
    o_k, o_v, o_qi, o_ki = offs
    qg = qg_ref[...]
    scale = A_DHEAD ** -0.5 * LOG2_E
    for h in range(A_HEADS):
        sl = slice(h * A_DHEAD, (h + 1) * A_DHEAD)
        qh = _rms_rows(p_ref[:, sl], qg) * scale
        qt_ref[0, 0, :, h * Q_BLOCK:(h + 1) * Q_BLOCK] = qh.T.astype(_BF)
    pair = LANES // IDX_DIM
    for g in range(IDX_HEADS // pair):
        slab = p_ref[:, o_qi + g * LANES:o_qi + (g + 1) * LANES].T
        for e in range(pair):
            h = g * pair + e
            qit_ref[0, 0, :, h * Q_BLOCK:(h + 1) * Q_BLOCK] = slab[e * IDX_DIM:(e + 1) * IDX_DIM, :].astype(_BF)
    k_ref[...] = _rms_rows(p_ref[:, o_k:o_v], kg_ref[...]).astype(_BF)
    ki_ref[...] = p_ref[:, o_ki:o_ki + IDX_DIM].astype(_BF)
    vt_ref[0, 0:A_DHEAD, :] = p_ref[:, o_v:o_qi].T.astype(_BF)
    ones_row = lax.broadcasted_iota(jnp.int32, (SUBLANES, Q_BLOCK), 0) == 0
    vt_ref[0, A_DHEAD:, :] = ones_row.astype(_F32).astype(_BF)


def _dsa_prep(proj, q_g, k_g, *, batch, seq, offs, name):
    n, width = proj.shape
    nb = seq // Q_BLOCK
    hq = A_HEADS * Q_BLOCK
    kern = functools.partial(_dsa_prep_kernel, offs=offs)
    return pl.pallas_call(
        kern,
        grid=(batch, nb),
        in_specs=[
            pl.BlockSpec((Q_BLOCK, width), lambda i, j: (i * nb + j, 0)),
            pl.BlockSpec((1, A_DHEAD), lambda i, j: (0, 0)),
            pl.BlockSpec((1, A_DHEAD), lambda i, j: (0, 0)),
        ],
        out_specs=[
            pl.BlockSpec((1, 1, A_DHEAD, hq), lambda i, j: (i, j, 0, 0)),
            pl.BlockSpec((1, 1, IDX_DIM, hq), lambda i, j: (i, j, 0, 0)),
            pl.BlockSpec((Q_BLOCK, A_DHEAD), lambda i, j: (i * nb + j, 0)),
            pl.BlockSpec((Q_BLOCK, IDX_DIM), lambda i, j: (i * nb + j, 0)),
            pl.BlockSpec((1, A_DHEAD + SUBLANES, Q_BLOCK), lambda i, j: (i, 0, j)),
        ],
        out_shape=[
            jax.ShapeDtypeStruct((batch, nb, A_DHEAD, hq), _BF),
            jax.ShapeDtypeStruct((batch, nb, IDX_DIM, hq), _BF),
            jax.ShapeDtypeStruct((n, A_DHEAD), _BF),
            jax.ShapeDtypeStruct((n, IDX_DIM), _BF),
            jax.ShapeDtypeStruct((batch, A_DHEAD + SUBLANES, seq), _BF),
        ],
        compiler_params=_params("parallel", "parallel"),
        name=name,
    )(proj, q_g.reshape(1, A_DHEAD), k_g.reshape(1, A_DHEAD))


def _dsa_kernel(bound_ref, ki_ref, k_ref, vt_ref, qit_ref, wi_ref, qt_ref, o_ref,
                key_ref, hi_ref, lo_ref, stat_ref, cut_ref, acc_ref, m_ref, alpha_ref, p_ref, *, tk, topk):
    q0 = pl.program_id(1) * Q_BLOCK
    n_kt = (q0 + Q_BLOCK - 1) // tk + 1
    n_bits_idx = (key_ref.shape[0] - 1).bit_length()
    lane_q = q0 + lax.broadcasted_iota(jnp.int32, (tk, Q_BLOCK), 1)
    row = lax.broadcasted_iota(jnp.int32, (tk, Q_BLOCK), 0)

    qit = qit_ref[0, 0]
    wi = wi_ref[0, 0] * ((IDX_HEADS * IDX_DIM) ** -0.5)

    def score_tile(kt, carry):
        start = pl.multiple_of(kt * tk, tk)
        lg = _dot(ki_ref[0, pl.ds(start, tk), :], qit)
        w = jnp.maximum(lg, 0.0) * wi
        s = w[:, 0:Q_BLOCK]
        for h in range(1, IDX_HEADS):
            s = s + w[:, h * Q_BLOCK:(h + 1) * Q_BLOCK]
        bits = pltpu.bitcast(s, jnp.int32)
        key = bits ^ ((bits >> 31) & 0x7FFFFFFF)
        key = jnp.where(bits == INT_MIN, 0, key)
        key = jnp.where(start + row <= lane_q, key, INT_MIN)
        key_ref[pl.ds(start, tk), :] = key
        return carry

    lax.fori_loop(0, n_kt, score_tile, 0)

    ck = 2 * tk
    n_ct = (n_kt + 1) // 2

    @pl.when(n_kt % 2 == 1)
    def _():
        key_ref[pl.ds(pl.multiple_of(n_kt * tk, tk), tk), :] = jnp.full((tk, Q_BLOCK), INT_MIN, jnp.int32)

    crow = lax.broadcasted_iota(jnp.int32, (ck, Q_BLOCK), 0)

    def count(pred):
        def body(ct, acc):
            start = pl.multiple_of(ct * ck, ck)
            hit = pred(key_ref[pl.ds(start, ck), :], start + crow).astype(jnp.int32)
            return acc + jnp.sum(hit.reshape(ck // SUBLANES, SUBLANES, Q_BLOCK), axis=0)
        acc = lax.fori_loop(0, n_ct, body, jnp.zeros((SUBLANES, Q_BLOCK), jnp.int32))
        return jnp.sum(acc, axis=0, keepdims=True)

    half = 1 << 15

    def split_tile(ct, carry):
        start = pl.multiple_of(ct * ck, ck)
        keys = key_ref[pl.ds(start, ck), :]
        hi_ref[pl.ds(start, ck), :] = (keys >> 16).astype(jnp.int16)
        lo_ref[pl.ds(start, ck), :] = ((keys & 0xFFFF) - half).astype(jnp.int16)
        return carry

    lax.fori_loop(0, n_ct, split_tile, 0)

    span = max(ck, SEARCH_SPAN)
    n_span = (n_kt * tk + span - 1) // span

    @pl.when(n_ct * ck < n_span * span)
    def _():
        start = pl.multiple_of(n_ct * ck, ck)
        hi_ref[pl.ds(start, ck), :] = jnp.full((ck, Q_BLOCK), -half, jnp.int16)
        lo_ref[pl.ds(start, ck), :] = jnp.full((ck, Q_BLOCK), -half, jnp.int16)

    def threshold_stats(rows):
        def count16(ref, pred):
            total = None
            for r0 in range(0, rows, ck):
                hit = pred(ref[r0:r0 + ck, :]).astype(jnp.int16)
                parts = [hit[t0:t0 + 2 * SUBLANES, :] for t0 in range(0, ck, 2 * SUBLANES)]
                while len(parts) > 1:
                    parts = [a + b for a, b in zip(parts[0::2], parts[1::2])]
                total = parts[0] if total is None else total + parts[0]
            return jnp.sum(total.astype(jnp.int32), axis=0, keepdims=True)

        def search16(ref, base_count):
            def bit(b, v):
                cand = v | lax.shift_left(jnp.int32(1), 15 - b)
                cand16 = (cand - half).astype(jnp.int16)
                cnt = base_count + count16(ref, lambda blk: blk >= cand16)
                return jnp.where(cnt >= topk, cand, v)
            return lax.fori_loop(0, 16, bit, jnp.zeros((1, Q_BLOCK), jnp.int32))

        thr_hi = search16(hi_ref, 0) - half
        thr_hi16 = thr_hi.astype(jnp.int16)
        next_hi16 = (thr_hi + 1).astype(jnp.int16)
        above = count16(hi_ref, lambda blk: blk >= next_hi16)
        above = jnp.where(thr_hi == half - 1, 0, above)
        for r0 in range(0, rows, ck):
            same = hi_ref[r0:r0 + ck, :] == thr_hi16
            lo_ref[r0:r0 + ck, :] = jnp.where(same, lo_ref[r0:r0 + ck, :], jnp.int16(-half))
        thr_lo = search16(lo_ref, above)
        thr_lo16 = (thr_lo - half).astype(jnp.int16)
        n_above = above + count16(lo_ref, lambda blk: blk > thr_lo16)
        n_same = count16(lo_ref, lambda blk: blk == thr_lo16)
        stat_ref[0:1, :] = (thr_hi << 16) | thr_lo
        stat_ref[1:2, :] = n_above
        stat_ref[2:3, :] = n_same

    for c in range(1, key_ref.shape[0] // span + 1):
        pl.when(n_span == c)(functools.partial(threshold_stats, c * span))

    thr = stat_ref[0:1, :]
    n_gt = stat_ref[1:2, :]
    n_eq = stat_ref[2:3, :]
    need = topk - n_gt
    cut_ref[...] = jnp.full((1, Q_BLOCK), 2 ** 30, jnp.int32)
    tied = jnp.logical_and(n_eq != need, thr != INT_MIN)

    @pl.when(jnp.max(tied.astype(jnp.int32)) > 0)
    def _():
        def cut_bit(b, p):
            cand = p | lax.shift_left(jnp.int32(1), n_bits_idx - 1 - b)
            cnt = count(lambda keys, idx: jnp.logical_and(keys == thr, idx < cand))
            return jnp.where(cnt < need, cand, p)
        cut_ref[...] = lax.fori_loop(0, n_bits_idx, cut_bit, jnp.zeros((1, Q_BLOCK), jnp.int32))

    cut = cut_ref[...]

    acc_ref[...] = jnp.zeros(acc_ref.shape, _F32)
    qt = qt_ref[0, 0]
    bound = bound_ref[0, 0]
    fixed_shift = 2.0 * bound <= MAX_FIXED_SHIFT_SPAN

    def selected(kt):
        start = pl.multiple_of(kt * tk, tk)
        keys = key_ref[pl.ds(start, tk), :]
        sel = jnp.logical_or(keys > thr, jnp.logical_and(keys == thr, start + row <= cut))
        return start, jnp.logical_and(sel, keys != INT_MIN)

    @pl.when(fixed_shift)
    def _():
        def att_tile(kt, carry):
            start, sel = selected(kt)
            mk = jnp.where(sel, -bound, NEG_BIG)
            lg = _dot(k_ref[0, pl.ds(start, tk), :], qt)
            for h in range(A_HEADS):
                sl = slice(h * Q_BLOCK, (h + 1) * Q_BLOCK)
                p_ref[:, sl] = jnp.exp2(lg[:, sl] + mk).astype(_BF)
            acc_ref[...] += _dot(vt_ref[0, :, pl.ds(start, tk)], p_ref[...])
            return carry

        lax.fori_loop(0, n_kt, att_tile, 0)

    @pl.when(jnp.logical_not(fixed_shift))
    def _():
        m_ref[...] = jnp.full(m_ref.shape, NEG_BIG, _F32)

        def att_tile(kt, carry):
            start, sel = selected(kt)
            mk = jnp.where(sel, 0.0, NEG_BIG)
            lg = _dot(k_ref[0, pl.ds(start, tk), :], qt)
            for h in range(A_HEADS):
                sl = slice(h * Q_BLOCK, (h + 1) * Q_BLOCK)
                x = lg[:, sl] + mk
                m_old = m_ref[:, sl]
                m_new = jnp.maximum(m_old, jnp.max(x, axis=0, keepdims=True))
                p_ref[:, sl] = jnp.exp2(x - m_new).astype(_BF)
                alpha_ref[:, sl] = jnp.exp2(m_old - m_new)
                m_ref[:, sl] = m_new
            pv = _dot(vt_ref[0, :, pl.ds(start, tk)], p_ref[...])
            acc_ref[...] = acc_ref[...] * alpha_ref[...] + pv
            return carry

        lax.fori_loop(0, n_kt, att_tile, 0)

    for h in range(A_HEADS):
        sl = slice(h * Q_BLOCK, (h + 1) * Q_BLOCK)
        out_h = acc_ref[0:A_DHEAD, sl] / acc_ref[A_DHEAD:A_DHEAD + 1, sl]
        o_ref[0, :, h * A_DHEAD:(h + 1) * A_DHEAD] = out_h.T.astype(o_ref.dtype)


def _dsa_attend(bound, ki, k, vt, qit, wi, qt, *, name):
    b, t, _ = ki.shape
    nb = t // Q_BLOCK
    hq = A_HEADS * Q_BLOCK
    topk = min(TOPK_MAX, t // 4)
    kern = functools.partial(_dsa_kernel, tk=KEY_TILE, topk=topk)
    return pl.pallas_call(
        kern,
        grid=(b, nb),
        in_specs=[
            pl.BlockSpec(memory_space=pltpu.SMEM),
            pl.BlockSpec((1, t, IDX_DIM), lambda i, j: (i, 0, 0)),
            pl.BlockSpec((1, t, A_DHEAD), lambda i, j: (i, 0, 0)),
            pl.BlockSpec((1, vt.shape[1], t), lambda i, j: (i, 0, 0)),
            pl.BlockSpec((1, 1, IDX_DIM, hq), lambda i, j: (i, j, 0, 0)),
            pl.BlockSpec((1, 1, 1, hq), lambda i, j: (i, j, 0, 0)),
            pl.BlockSpec((1, 1, A_DHEAD, hq), lambda i, j: (i, j, 0, 0)),
        ],
        out_specs=pl.BlockSpec((1, Q_BLOCK, A_HEADS * A_DHEAD), lambda i, j: (i, j, 0)),
        out_shape=jax.ShapeDtypeStruct((b, t, A_HEADS * A_DHEAD), _BF),
        scratch_shapes=[
            pltpu.VMEM((t, Q_BLOCK), jnp.int32),
            pltpu.VMEM((t, Q_BLOCK), jnp.int16),
            pltpu.VMEM((t, Q_BLOCK), jnp.int16),
            pltpu.VMEM((SUBLANES, Q_BLOCK), jnp.int32),
            pltpu.VMEM((1, Q_BLOCK), jnp.int32),
            pltpu.VMEM((vt.shape[1], hq), _F32),
            pltpu.VMEM((1, hq), _F32),
            pltpu.VMEM((1, hq), _F32),
            pltpu.VMEM((KEY_TILE, hq), _BF),
        ],
        compiler_params=_params("parallel", "arbitrary"),
        name=name,
    )(bound.reshape(1, 1), ki, k, vt, qit, wi, qt)


def _mlstm_layer(x2, norm_g, w_in_bf, w_gate, gate_b, head_g, w_out_bf, *, layer, batch, seq):
    n, d = x2.shape
    hk = M_HEADS * M_DQK
    hv = M_HEADS * M_DV
    main = 2 * hk + 2 * hv
    proj = _norm_matmul(x2, norm_g, w_in_bf, layer=layer, cols=main, tm=1024, tn=1024, name="mlstm_in")
    gates = _gates(x2, norm_g, w_gate, gate_b, tm=512, name="mlstm_gates")
    gates = gates.reshape(batch, seq, 2 * M_HEADS)
    h = _mlstm_core(proj.reshape(batch, seq, main), gates, gates.transpose(0, 2, 1), head_g, name="mlstm_core")
    return _matmul_resid(h.reshape(n, hv), w_out_bf, x2, layer=layer, tm=512, tn=d, name="mlstm_out")


def _dsa_layer(x2, norm_g, w_in, q_g, k_g, w_out, *, batch, seq):
    n, d = x2.shape
    qd = A_HEADS * A_DHEAD
    qid = IDX_HEADS * IDX_DIM
    o_k, o_v, o_qi = qd, qd + A_DHEAD, qd + 2 * A_DHEAD
    o_ki, o_wi, o_end = o_qi + qid, o_qi + qid + IDX_DIM, o_qi + qid + IDX_DIM + IDX_HEADS
    tn = 512
    padded = -(-o_end // tn) * tn
    w_pad = jnp.zeros((1, d, padded), _BF).at[0, :, :o_end].set(w_in.astype(_BF))
    proj = _norm_matmul(x2, norm_g, w_pad, layer=0, cols=padded, tm=1024, tn=tn, name="dsa_in")
    qt, qit, kn, ki, vt = _dsa_prep(proj, q_g, k_g, batch=batch, seq=seq, offs=(o_k, o_v, o_qi, o_ki), name="dsa_qknorm")
    nb = seq // Q_BLOCK
    wi = proj[:, o_wi:o_end].reshape(batch, nb, Q_BLOCK, IDX_HEADS).transpose(0, 1, 3, 2).reshape(batch, nb, 1, IDX_HEADS * Q_BLOCK)
    bound = 1.01 * LOG2_E * A_DHEAD ** 0.5 * jnp.max(jnp.abs(q_g)) * jnp.max(jnp.abs(k_g))
    out = _dsa_attend(bound.astype(_F32), ki.reshape(batch, seq, IDX_DIM), kn.reshape(batch, seq, A_DHEAD), vt, qit, wi, qt, name="dsa_attend")
    return _matmul_resid(out.reshape(n, qd), w_out.astype(_BF)[None], x2, layer=0, tm=512, tn=d, name="dsa_out")


def _conv_layer(x2, norm_g, w_in, dw_w, dw_b, ln_g, ln_b, w_out, *, seq):
    u = _norm_glu(x2, norm_g, w_in.astype(_BF), tm=1024, tn=512, name="conv_in")
    v = _conv_mid(u, dw_w, dw_b, ln_g, ln_b, seq=seq, tm=256, name="conv_mid")
    return _matmul_resid(v, w_out.astype(_BF)[None], x2, layer=0, tm=512, tn=x2.shape[1], name="conv_out")


def kernel(x, mix_norm_g, ffn_norm_g, mlstm_w_in, mlstm_gate_b, mlstm_head_g, mlstm_w_out,
           dsa_w_in, dsa_q_g, dsa_k_g, dsa_w_out, conv_w_in, conv_dw_w, conv_dw_b,
           conv_ln_g, conv_ln_b, conv_w_out, ffn_w_up, ffn_dw_w, ffn_w_down):
    batch, seq, d = x.shape
    depth = mix_norm_g.shape[0]
    x2 = x.reshape(batch * seq, d)
    main = 2 * M_HEADS * (M_DQK + M_DV)
    mlstm_w_in_bf = mlstm_w_in.astype(_BF)
    mlstm_w_out_bf = mlstm_w_out.astype(_BF)
    ffn_w_up_bf = ffn_w_up.astype(_BF)
    ffn_w_down_bf = ffn_w_down.astype(_BF)
    for i in range(depth):
        kind = i % N_MIXERS
        j = i // N_MIXERS
        if kind == 0:
            x2 = _mlstm_layer(x2, mix_norm_g[i], mlstm_w_in_bf, mlstm_w_in[j, :, main:], mlstm_gate_b[j], mlstm_head_g[j],
                              mlstm_w_out_bf, layer=j, batch=batch, seq=seq)
        elif kind == 1:
            x2 = _dsa_layer(x2, mix_norm_g[i], dsa_w_in[j], dsa_q_g[j], dsa_k_g[j], dsa_w_out[j], batch=batch, seq=seq)
        else:
            x2 = _conv_layer(x2, mix_norm_g[i], conv_w_in[j], conv_dw_w[j], conv_dw_b[j], conv_ln_g[j], conv_ln_b[j],
                             conv_w_out[j], seq=seq)
        x2 = _ffn(x2, ffn_norm_g[i], ffn_w_up_bf, ffn_dw_w[i], ffn_w_down_bf, layer=i, seq=seq, tm=512, tn=512, name="ffn")
    return x2.reshape(batch, seq, d)
```

```python
import functools

import jax
import jax.numpy as jnp
from jax import lax
from jax.experimental import pallas as pl
from jax.experimental.pallas import tpu as pltpu

N_MIXERS = 3
EPS = 1e-6
M_HEADS = 8
M_DQK = 128
M_DV = 256
A_HEADS = 16
A_DHEAD = 128
IDX_HEADS = 16
IDX_DIM = 64
TOPK_MAX = 256
CONV_WIDTH = 31
FFN_CONV_WIDTH = 3

LANES = 128
SUBLANES = 8
FFN_HALO = 16
VMEM_LIMIT = 52 * 1024 * 1024
MLSTM_CHUNK = 256
Q_BLOCK = 128
KEY_TILE = 512
SEARCH_SPAN = 512
CONV_HALO = 32
LOG2_E = 1.4426950408889634
INT_MIN = -(2 ** 31)
NEG_BIG = -1e30
MAX_FIXED_SHIFT_SPAN = 120.0

_BF = jnp.bfloat16
_F32 = jnp.float32


def _params(*sem):
    return pltpu.CompilerParams(dimension_semantics=sem, vmem_limit_bytes=VMEM_LIMIT)


def _rms_rows(x, g):
    ms = jnp.mean(x * x, axis=-1, keepdims=True)
    return x * lax.rsqrt(ms + EPS) * g


def _dot(a, b):
    return jnp.dot(a, b, preferred_element_type=_F32)


def _split3(x):
    hi = x.astype(_BF)
    r1 = x - hi.astype(_F32)
    mid = r1.astype(_BF)
    lo = (r1 - mid.astype(_F32)).astype(_BF)
    return hi, mid, lo


def _norm_matmul_kernel(x_ref, g_ref, w_ref, o_ref, xn_ref):
    @pl.when(pl.program_id(1) == 0)
    def _():
        xn_ref[...] = _rms_rows(x_ref[...], g_ref[...]).astype(_BF)

    o_ref[...] = _dot(xn_ref[...], w_ref[...]).astype(o_ref.dtype)


def _norm_matmul(x, g, w, *, layer, cols, tm, tn, name):
    n, d = x.shape
    m = cols
    return pl.pallas_call(
        _norm_matmul_kernel,
        grid=(n // tm, m // tn),
        in_specs=[
            pl.BlockSpec((tm, d), lambda i, j: (i, 0)),
            pl.BlockSpec((1, d), lambda i, j: (0, 0)),
            pl.BlockSpec((None, d, tn), lambda i, j: (layer, 0, j)),
        ],
        out_specs=pl.BlockSpec((tm, tn), lambda i, j: (i, j)),
        out_shape=jax.ShapeDtypeStruct((n, m), _F32),
        scratch_shapes=[pltpu.VMEM((tm, d), _BF)],
        compiler_params=_params("parallel", "arbitrary"),
        name=name,
    )(x, g.reshape(1, d), w)


def _norm_glu_kernel(x_ref, g_ref, wa_ref, wg_ref, o_ref, xn_ref):
    @pl.when(pl.program_id(1) == 0)
    def _():
        xn_ref[...] = _rms_rows(x_ref[...], g_ref[...]).astype(_BF)

    xn = xn_ref[...]
    a = _dot(xn, wa_ref[...])
    gate = _dot(xn, wg_ref[...])
    o_ref[...] = a * jax.nn.sigmoid(gate)


def _norm_glu(x, g, w, *, tm, tn, name):
    n, d = x.shape
    half = w.shape[1] // 2
    nj = half // tn
    return pl.pallas_call(
        _norm_glu_kernel,
        grid=(n // tm, nj),
        in_specs=[
            pl.BlockSpec((tm, d), lambda i, j: (i, 0)),
            pl.BlockSpec((1, d), lambda i, j: (0, 0)),
            pl.BlockSpec((d, tn), lambda i, j: (0, j)),
            pl.BlockSpec((d, tn), lambda i, j: (0, j + nj)),
        ],
        out_specs=pl.BlockSpec((tm, tn), lambda i, j: (i, j)),
        out_shape=jax.ShapeDtypeStruct((n, half), _F32),
        scratch_shapes=[pltpu.VMEM((tm, d), _BF)],
        compiler_params=_params("parallel", "arbitrary"),
        name=name,
    )(x, g.reshape(1, d), w, w)


def _matmul_resid_kernel(a_ref, w_ref, r_ref, o_ref):
    o_ref[...] = r_ref[...] + _dot(a_ref[...], w_ref[...])


def _matmul_resid(a, w, resid, *, layer, tm, tn, name):
    n, k = a.shape
    m = w.shape[2]
    return pl.pallas_call(
        _matmul_resid_kernel,
        grid=(n // tm, m // tn),
        in_specs=[
            pl.BlockSpec((tm, k), lambda i, j: (i, 0)),
            pl.BlockSpec((None, k, tn), lambda i, j: (layer, 0, j)),
            pl.BlockSpec((tm, tn), lambda i, j: (i, j)),
        ],
        out_specs=pl.BlockSpec((tm, tn), lambda i, j: (i, j)),
        out_shape=jax.ShapeDtypeStruct((n, m), _F32),
        compiler_params=_params("parallel", "arbitrary"),
        name=name,
    )(a, w, resid)


def _ffn_kernel(x_ref, xh_ref, g_ref, wa_ref, wb_ref, dwa_ref, dwb_ref, wd_ref, o_ref,
                xn_ref, ha0_ref, hb0_ref, ha1_ref, hb1_ref, act_ref, *, tm, nj, total, tiles_per_seq, act_rows):
    s = pl.program_id(0)

    @pl.when(s == 0)
    def _():
        ha1_ref[...] = jnp.zeros_like(ha1_ref)
        hb1_ref[...] = jnp.zeros_like(hb1_ref)
        o_ref[...] = jnp.zeros_like(o_ref)

    @pl.when(jnp.logical_and(s % nj == 0, s < total))
    def _():
        g = g_ref[...]
        halo = _rms_rows(xh_ref[...], g)
        halo = jnp.where((s // nj) % tiles_per_seq == 0, 0.0, halo)
        xn_ref[0:FFN_HALO, :] = halo.astype(_BF)
        xn_ref[FFN_HALO:, :] = _rms_rows(x_ref[...], g).astype(_BF)

    @pl.when((s - 1) % nj == 0)
    def _():
        o_ref[...] = x_ref[...]

    def conv(h_ref, dw, r0, rows):
        base = r0 + FFN_HALO - (FFN_CONV_WIDTH - 1)
        out = dw[0:1, :] * h_ref[base:base + rows, :]
        for k in range(1, FFN_CONV_WIDTH):
            out = out + dw[k:k + 1, :] * h_ref[base + k:base + k + rows, :]
        return out

    def step(ha_prev, hb_prev, ha_next, hb_next):
        ha_next[...] = _dot(xn_ref[...], wa_ref[...])
        hb_next[...] = _dot(xn_ref[...], wb_ref[...])
        dwa = dwa_ref[...]
        dwb = dwb_ref[...]
        for r0 in range(0, tm, act_rows):
            a = conv(ha_prev, dwa, r0, act_rows)
            b = conv(hb_prev, dwb, r0, act_rows)
            act_ref[r0:r0 + act_rows, :] = (a * jax.nn.sigmoid(a) * b).astype(_BF)
        o_ref[...] += _dot(act_ref[...], wd_ref[...])

    @pl.when(s % 2 == 0)
    def _():
        step(ha1_ref, hb1_ref, ha0_ref, hb0_ref)

    @pl.when(s % 2 == 1)
    def _():
        step(ha0_ref, hb0_ref, ha1_ref, hb1_ref)


def _ffn(x, g, w_up, dw, w_down, *, layer, seq, tm, tn, name):
    n, d = x.shape
    f = w_up.shape[2] // 2
    nj = f // tn
    total = (n // tm) * nj
    hblk = tm // FFN_HALO
    kern = functools.partial(_ffn_kernel, tm=tm, nj=nj, total=total, tiles_per_seq=seq // tm, act_rows=64)

    def cur(s):
        c = jnp.minimum(s, total - 1)
        return c // nj, c % nj

    def prev(s):
        p = jnp.maximum(s - 1, 0)
        return p // nj, p % nj

    h_scratch = pltpu.VMEM((tm + FFN_HALO, tn), _F32)
    return pl.pallas_call(
        kern,
        grid=(total + 1,),
        in_specs=[
            pl.BlockSpec((tm, d), lambda s: (cur(s)[0], 0)),
            pl.BlockSpec((FFN_HALO, d), lambda s: (jnp.maximum(cur(s)[0] * hblk - 1, 0), 0)),
            pl.BlockSpec((1, d), lambda s: (0, 0)),
            pl.BlockSpec((None, d, tn), lambda s: (layer, 0, cur(s)[1])),
            pl.BlockSpec((None, d, tn), lambda s: (layer, 0, cur(s)[1] + nj)),
            pl.BlockSpec((FFN_CONV_WIDTH, tn), lambda s: (0, prev(s)[1])),
            pl.BlockSpec((FFN_CONV_WIDTH, tn), lambda s: (0, prev(s)[1] + nj)),
            pl.BlockSpec((None, tn, d), lambda s: (layer, prev(s)[1], 0)),
        ],
        out_specs=pl.BlockSpec((tm, d), lambda s: (prev(s)[0], 0)),
        out_shape=jax.ShapeDtypeStruct((n, d), _F32),
        scratch_shapes=[pltpu.VMEM((tm + FFN_HALO, d), _BF), h_scratch, h_scratch, h_scratch, h_scratch,
                        pltpu.VMEM((tm, tn), _BF)],
        compiler_params=_params("arbitrary"),
        name=name,
    )(x, x, g.reshape(1, d), w_up, w_up, dw, dw, w_down)


def _conv_mid_kernel(u_ref, uh_ref, dw_ref, db_ref, lg_ref, lb_ref, o_ref, ext_ref, c_ref, sh_ref, *, tm, tiles_per_seq, rb, cb):
    i = pl.program_id(0)
    d = u_ref.shape[1]
    ext_ref[0:CONV_HALO, :] = jnp.where(i % tiles_per_seq == 0, 0.0, uh_ref[...])
    ext_ref[CONV_HALO:, :] = u_ref[...]
    base = CONV_HALO - (CONV_WIDTH - 1)

    def col_block(c, carry):
        cs = pl.multiple_of(c * cb, cb)
        for r in range(tm // rb):
            acc = jnp.zeros((rb, cb), _F32) + db_ref[:, pl.ds(cs, cb)]
            for phase in range(SUBLANES):
                taps = [k for k in range(CONV_WIDTH) if (base + k) % SUBLANES == phase]
                span = rb + max(base + k - phase for k in taps)
                sh_ref[0:span, :] = ext_ref[r * rb + phase:r * rb + phase + span, pl.ds(cs, cb)]
                for k in taps:
                    off = base + k - phase
                    acc = acc + sh_ref[off:off + rb, :] * dw_ref[k:k + 1, pl.ds(cs, cb)]
            c_ref[r * rb:(r + 1) * rb, pl.ds(cs, cb)] = acc
        return carry

    lax.fori_loop(0, d // cb, col_block, 0)
    c = c_ref[...]
    mu = jnp.mean(c, axis=-1, keepdims=True)
    xc = c - mu
    var = jnp.mean(xc * xc, axis=-1, keepdims=True)
    y = xc * lax.rsqrt(var + EPS) * lg_ref[...] + lb_ref[...]
    o_ref[...] = (y * jax.nn.sigmoid(y)).astype(o_ref.dtype)


def _conv_mid(u, dw, db, lg, lb, *, seq, tm, name):
    n, d = u.shape
    hblk = tm // CONV_HALO
    rb, cb = 64, 256
    kern = functools.partial(_conv_mid_kernel, tm=tm, tiles_per_seq=seq // tm, rb=rb, cb=cb)
    return pl.pallas_call(
        kern,
        grid=(n // tm,),
        in_specs=[
            pl.BlockSpec((tm, d), lambda i: (i, 0)),
            pl.BlockSpec((CONV_HALO, d), lambda i: (jnp.maximum(i * hblk - 1, 0), 0)),
            pl.BlockSpec((CONV_WIDTH, d), lambda i: (0, 0)),
            pl.BlockSpec((1, d), lambda i: (0, 0)),
            pl.BlockSpec((1, d), lambda i: (0, 0)),
            pl.BlockSpec((1, d), lambda i: (0, 0)),
        ],
        out_specs=pl.BlockSpec((tm, d), lambda i: (i, 0)),
        out_shape=jax.ShapeDtypeStruct((n, d), _BF),
        scratch_shapes=[pltpu.VMEM((tm + CONV_HALO, d), _F32), pltpu.VMEM((tm, d), _F32),
                        pltpu.VMEM((rb + CONV_HALO, cb), _F32)],
        compiler_params=_params("parallel"),
        name=name,
    )(u, u, dw, db.reshape(1, d), lg.reshape(1, d), lb.reshape(1, d))


def _gates_kernel(x_ref, g_ref, whi_ref, wlo_ref, b_ref, o_ref):
    xn = _rms_rows(x_ref[...], g_ref[...])
    xhi = xn.astype(_BF)
    xlo = (xn - xhi.astype(_F32)).astype(_BF)
    whi = whi_ref[...]
    z = _dot(xhi, whi) + _dot(xlo, whi) + _dot(xhi, wlo_ref[...]) + b_ref[...]
    log_f = jnp.minimum(z, 0.0) - jnp.log(1.0 + jnp.exp(-jnp.abs(z)))
    col = lax.broadcasted_iota(jnp.int32, z.shape, 1)
    o_ref[...] = jnp.where(col < M_HEADS, z, log_f)


def _gates(x, g, w_gate, bias, *, tm, name):
    n, d = x.shape
    ng = w_gate.shape[1]
    wp = jnp.zeros((d, LANES), _F32).at[:, :ng].set(w_gate)
    whi = wp.astype(_BF)
    wlo = (wp - whi.astype(_F32)).astype(_BF)
    bp = jnp.zeros((1, LANES), _F32).at[0, :ng].set(bias)
    out = pl.pallas_call(
        _gates_kernel,
        grid=(n // tm,),
        in_specs=[
            pl.BlockSpec((tm, d), lambda i: (i, 0)),
            pl.BlockSpec((1, d), lambda i: (0, 0)),
            pl.BlockSpec((d, LANES), lambda i: (0, 0)),
            pl.BlockSpec((d, LANES), lambda i: (0, 0)),
            pl.BlockSpec((1, LANES), lambda i: (0, 0)),
        ],
        out_specs=pl.BlockSpec((tm, LANES), lambda i: (i, 0)),
        out_shape=jax.ShapeDtypeStruct((n, LANES), _F32),
        compiler_params=_params("parallel"),
        name=name,
    )(x, g.reshape(1, d), whi, wlo, bp)
    return out[:, :ng]


def _mlstm_kernel(q_ref, k_ref, v_ref, o_ref, gc_ref, gr_ref, hg_ref, out_ref, c_ref, n_ref, m_ref, *, chunk):
    ln = chunk
    scale = M_DQK ** -0.5

    @pl.when(pl.program_id(1) == 0)
    def _():
        c_ref[...] = jnp.zeros_like(c_ref)
        n_ref[...] = jnp.zeros_like(n_ref)
        m_ref[...] = jnp.zeros_like(m_ref)

    gc = gc_ref[0]
    gr = gr_ref[0]
    rows = lax.broadcasted_iota(jnp.int32, (ln, ln), 0)
    cols = lax.broadcasted_iota(jnp.int32, (ln, ln), 1)
    causal = rows >= cols
    tril = causal.astype(_BF)
    triu = (rows <= cols).astype(_BF)
    cum_col = sum(_dot(tril, p) for p in _split3(gc))
    cum_row = sum(_dot(p, triu) for p in _split3(gr))

    for h in range(M_HEADS):
        f = M_HEADS + h
        bcol = cum_col[:, f:f + 1]
        brow = cum_row[f:f + 1, :]
        icol = gc[:, h:h + 1]
        irow = gr[h:h + 1, :]
        m_prev = m_ref[h, 0:1, 0:1]
        nrow = n_ref[h, 0:1, :]
        qh = q_ref[0, :, h * M_DQK:(h + 1) * M_DQK]
        kh = k_ref[0, :, h * M_DQK:(h + 1) * M_DQK]
        vb = v_ref[0, :, h * M_DV:(h + 1) * M_DV].astype(_BF)
        qb = qh.astype(_BF)

        dmat = jnp.where(causal, bcol - brow + irow, -jnp.inf)
        inter = bcol + m_prev
        m_t = jnp.maximum(inter, jnp.max(dmat, axis=1, keepdims=True))
        w_intra = jnp.exp(dmat - m_t)
        w_inter = jnp.exp(inter - m_t)
        s = lax.dot_general(qb, kh.astype(_BF), (((1,), (1,)), ((), ())), preferred_element_type=_F32)
        s = s * (w_intra * scale)
        qc = _dot(qb, c_ref[h].astype(_BF)) * scale
        num = w_inter * qc + _dot(s.astype(_BF), vb)
        qn = jnp.sum(qh * nrow, axis=1, keepdims=True) * scale
        den = w_inter * qn + jnp.sum(s, axis=1, keepdims=True)
        hh = num / jnp.maximum(jnp.abs(den), jnp.exp(-m_t))

        bl = bcol[ln - 1:ln, :]
        g_col = bl - bcol + icol
        g_row = bl - brow + irow
        m_new = jnp.maximum(bl + m_prev, jnp.max(g_row, axis=1, keepdims=True))
        a_prev = jnp.exp(bl + m_prev - m_new)
        kw = kh * jnp.exp(g_col - m_new)
        upd = lax.dot_general(kw.astype(_BF), vb, (((0,), (0,)), ((), ())), preferred_element_type=_F32)
        c_ref[h] = a_prev * c_ref[h] + upd
        n_new = a_prev * nrow + jnp.sum(kw, axis=0, keepdims=True)
        n_ref[h] = jnp.broadcast_to(n_new, (SUBLANES, M_DQK))
        m_ref[h] = jnp.broadcast_to(m_new, (SUBLANES, LANES))

        hn = hh * lax.rsqrt(jnp.mean(hh * hh, axis=1, keepdims=True) + EPS) * hg_ref[:, h * M_DV:(h + 1) * M_DV]
        og = o_ref[0, :, h * M_DV:(h + 1) * M_DV]
        out_ref[0, :, h * M_DV:(h + 1) * M_DV] = (hn * jax.nn.sigmoid(og)).astype(out_ref.dtype)


def _mlstm_core(proj, gates_col, gates_row, head_g, *, name):
    b, t, _ = proj.shape
    hk = M_HEADS * M_DQK
    hv = M_HEADS * M_DV
    ln = MLSTM_CHUNK
    ng = 2 * M_HEADS
    kern = functools.partial(_mlstm_kernel, chunk=ln)
    return pl.pallas_call(
        kern,
        grid=(b, t // ln),
        in_specs=[
            pl.BlockSpec((1, ln, hk), lambda i, c: (i, c, 0)),
            pl.BlockSpec((1, ln, hk), lambda i, c: (i, c, 1)),
            pl.BlockSpec((1, ln, hv), lambda i, c: (i, c, (2 * hk) // hv)),
            pl.BlockSpec((1, ln, hv), lambda i, c: (i, c, (2 * hk) // hv + 1)),
            pl.BlockSpec((1, ln, ng), lambda i, c: (i, c, 0)),
            pl.BlockSpec((1, ng, ln), lambda i, c: (i, 0, c)),
            pl.BlockSpec((1, hv), lambda i, c: (0, 0)),
        ],
        out_specs=pl.BlockSpec((1, ln, hv), lambda i, c: (i, c, 0)),
        out_shape=jax.ShapeDtypeStruct((b, t, hv), _BF),
        scratch_shapes=[
            pltpu.VMEM((M_HEADS, M_DQK, M_DV), _F32),
            pltpu.VMEM((M_HEADS, SUBLANES, M_DQK), _F32),
            pltpu.VMEM((M_HEADS, SUBLANES, LANES), _F32),
        ],
        compiler_params=_params("parallel", "arbitrary"),
        name=name,
    )(proj, proj, proj, proj, gates_col, gates_row, head_g.reshape(1, hv))


def _dsa_prep_kernel(p_ref, qg_ref, kg_ref, qt_ref, qit_ref, k_ref, ki_ref, vt_ref, *, offs):
    o_k, o_v, o_qi, o_ki = offs
    qg = qg_ref[...]
    scale = A_DHEAD ** -0.5 * LOG2_E
    for h in range(A_HEADS):
        sl = slice(h * A_DHEAD, (h + 1) * A_DHEAD)
        qh = _rms_rows(p_ref[:, sl], qg) * scale
        qt_ref[0, 0, :, h * Q_BLOCK:(h + 1) * Q_BLOCK] = qh.T.astype(_BF)
    pair = LANES // IDX_DIM
    for g in range(IDX_HEADS // pair):
        slab = p_ref[:, o_qi + g * LANES:o_qi + (g + 1) * LANES].T
        for e in range(pair):
            h = g * pair + e
            qit_ref[0, 0, :, h * Q_BLOCK:(h + 1) * Q_BLOCK] = slab[e * IDX_DIM:(e + 1) * IDX_DIM, :].astype(_BF)
    k_ref[...] = _rms_rows(p_ref[:, o_k:o_v], kg_ref[...]).astype(_BF)
    ki_ref[...] = p_ref[:, o_ki:o_ki + IDX_DIM].astype(_BF)
    vt_ref[0, 0:A_DHEAD, :] = p_ref[:, o_v:o_qi].T.astype(_BF)
    ones_row = lax.broadcasted_iota(jnp.int32, (SUBLANES, Q_BLOCK), 0) == 0
    vt_ref[0, A_DHEAD:, :] = ones_row.astype(_F32).astype(_BF)


def _dsa_prep(proj, q_g, k_g, *, batch, seq, offs, name):
    n, width = proj.shape
    nb = seq // Q_BLOCK
    hq = A_HEADS * Q_BLOCK
    kern = functools.partial(_dsa_prep_kernel, offs=offs)
    return pl.pallas_call(
        kern,
        grid=(batch, nb),
        in_specs=[
            pl.BlockSpec((Q_BLOCK, width), lambda i, j: (i * nb + j, 0)),
            pl.BlockSpec((1, A_DHEAD), lambda i, j: (0, 0)),
            pl.BlockSpec((1, A_DHEAD), lambda i, j: (0, 0)),
        ],
        out_specs=[
            pl.BlockSpec((1, 1, A_DHEAD, hq), lambda i, j: (i, j, 0, 0)),
            pl.BlockSpec((1, 1, IDX_DIM, hq), lambda i, j: (i, j, 0, 0)),
            pl.BlockSpec((Q_BLOCK, A_DHEAD), lambda i, j: (i * nb + j, 0)),
            pl.BlockSpec((Q_BLOCK, IDX_DIM), lambda i, j: (i * nb + j, 0)),
            pl.BlockSpec((1, A_DHEAD + SUBLANES, Q_BLOCK), lambda i, j: (i, 0, j)),
        ],
        out_shape=[
            jax.ShapeDtypeStruct((batch, nb, A_DHEAD, hq), _BF),
            jax.ShapeDtypeStruct((batch, nb, IDX_DIM, hq), _BF),
            jax.ShapeDtypeStruct((n, A_DHEAD), _BF),
            jax.ShapeDtypeStruct((n, IDX_DIM), _BF),
            jax.ShapeDtypeStruct((batch, A_DHEAD + SUBLANES, seq), _BF),
        ],
        compiler_params=_params("parallel", "parallel"),
        name=name,
    )(proj, q_g.reshape(1, A_DHEAD), k_g.reshape(1, A_DHEAD))


def _dsa_kernel(bound_ref, ki_ref, k_ref, vt_ref, qit_ref, wi_ref, qt_ref, o_ref,
                key_ref, hi_ref, lo_ref, stat_ref, cut_ref, acc_ref, m_ref, alpha_ref, p_ref, *, tk, topk):
    q0 = pl.program_id(1) * Q_BLOCK
    n_kt = (q0 + Q_BLOCK - 1) // tk + 1
    n_bits_idx = (key_ref.shape[0] - 1).bit_length()
    lane_q = q0 + lax.broadcasted_iota(jnp.int32, (tk, Q_BLOCK), 1)
    row = lax.broadcasted_iota(jnp.int32, (tk, Q_BLOCK), 0)

    qit = qit_ref[0, 0]
    wi = wi_ref[0, 0] * ((IDX_HEADS * IDX_DIM) ** -0.5)

    def score_tile(kt, carry):
        start = pl.multiple_of(kt * tk, tk)
        lg = _dot(ki_ref[0, pl.ds(start, tk), :], qit)
        w = jnp.maximum(lg, 0.0) * wi
        s = w[:, 0:Q_BLOCK]
        for h in range(1, IDX_HEADS):
            s = s + w[:, h * Q_BLOCK:(h + 1) * Q_BLOCK]
        bits = pltpu.bitcast(s, jnp.int32)
        key = bits ^ ((bits >> 31) & 0x7FFFFFFF)
        key = jnp.where(bits == INT_MIN, 0, key)
        key = jnp.where(start + row <= lane_q, key, INT_MIN)
        key_ref[pl.ds(start, tk), :] = key
        return carry

    lax.fori_loop(0, n_kt, score_tile, 0)

    ck = 2 * tk
    n_ct = (n_kt + 1) // 2

    @pl.when(n_kt % 2 == 1)
    def _():
        key_ref[pl.ds(pl.multiple_of(n_kt * tk, tk), tk), :] = jnp.full((tk, Q_BLOCK), INT_MIN, jnp.int32)

    crow = lax.broadcasted_iota(jnp.int32, (ck, Q_BLOCK), 0)

    def count(pred):
        def body(ct, acc):
            start = pl.multiple_of(ct * ck, ck)
            hit = pred(key_ref[pl.ds(start, ck), :], start + crow).astype(jnp.int32)
            return acc + jnp.sum(hit.reshape(ck // SUBLANES, SUBLANES, Q_BLOCK), axis=0)
        acc = lax.fori_loop(0, n_ct, body, jnp.zeros((SUBLANES, Q_BLOCK), jnp.int32))
        return jnp.sum(acc, axis=0, keepdims=True)

    half = 1 << 15

    def split_tile(ct, carry):
        start = pl.multiple_of(ct * ck, ck)
        keys = key_ref[pl.ds(start, ck), :]
        hi_ref[pl.ds(start, ck), :] = (keys >> 16).astype(jnp.int16)
        lo_ref[pl.ds(start, ck), :] = ((keys & 0xFFFF) - half).astype(jnp.int16)
        return carry

    lax.fori_loop(0, n_ct, split_tile, 0)

    span = SEARCH_SPAN
    n_span = (n_kt * tk + span - 1) // span

    @pl.when(n_ct * ck < n_span * span)
    def _():
        start = pl.multiple_of(n_ct * ck, ck)
        hi_ref[pl.ds(start, ck), :] = jnp.full((ck, Q_BLOCK), -half, jnp.int16)
        lo_ref[pl.ds(start, ck), :] = jnp.full((ck, Q_BLOCK), -half, jnp.int16)

    def threshold_stats(rows, ck=min(ck, span)):
        def count16(ref, pred):
            total = None
            for r0 in range(0, rows, ck):
                hit = pred(ref[r0:r0 + ck, :]).astype(jnp.int16)
                parts = [hit[t0:t0 + 2 * SUBLANES, :] for t0 in range(0, ck, 2 * SUBLANES)]
                while len(parts) > 1:
                    parts = [a + b for a, b in zip(parts[0::2], parts[1::2])]
                total = parts[0] if total is None else total + parts[0]
            return jnp.sum(total.astype(jnp.int32), axis=0, keepdims=True)

        def search16(ref, base_count):
            def bit(b, v):
                cand = v | lax.shift_left(jnp.int32(1), 15 - b)
                cand16 = (cand - half).astype(jnp.int16)
                cnt = base_count + count16(ref, lambda blk: blk >= cand16)
                return jnp.where(cnt >= topk, cand, v)
            return lax.fori_loop(0, 16, bit, jnp.zeros((1, Q_BLOCK), jnp.int32))

        thr_hi = search16(hi_ref, 0) - half
        thr_hi16 = thr_hi.astype(jnp.int16)
        next_hi16 = (thr_hi + 1).astype(jnp.int16)
        above = count16(hi_ref, lambda blk: blk >= next_hi16)
        above = jnp.where(thr_hi == half - 1, 0, above)
        for r0 in range(0, rows, ck):
            same = hi_ref[r0:r0 + ck, :] == thr_hi16
            lo_ref[r0:r0 + ck, :] = jnp.where(same, lo_ref[r0:r0 + ck, :], jnp.int16(-half))
        thr_lo = search16(lo_ref, above)
        thr_lo16 = (thr_lo - half).astype(jnp.int16)
        n_above = above + count16(lo_ref, lambda blk: blk > thr_lo16)
        n_same = count16(lo_ref, lambda blk: blk == thr_lo16)
        stat_ref[0:1, :] = (thr_hi << 16) | thr_lo
        stat_ref[1:2, :] = n_above
        stat_ref[2:3, :] = n_same

    for c in range(1, key_ref.shape[0] // span + 1):
        pl.when(n_span == c)(functools.partial(threshold_stats, c * span))

    thr = stat_ref[0:1, :]
    n_gt = stat_ref[1:2, :]
    n_eq = stat_ref[2:3, :]
    need = topk - n_gt
    cut_ref[...] = jnp.full((1, Q_BLOCK), 2 ** 30, jnp.int32)
    tied = jnp.logical_and(n_eq != need, thr != INT_MIN)

    @pl.when(jnp.max(tied.astype(jnp.int32)) > 0)
    def _():
        def cut_bit(b, p):
            cand = p | lax.shift_left(jnp.int32(1), n_bits_idx - 1 - b)
            cnt = count(lambda keys, idx: jnp.logical_and(keys == thr, idx < cand))
            return jnp.where(cnt < need, cand, p)
        cut_ref[...] = lax.fori_loop(0, n_bits_idx, cut_bit, jnp.zeros((1, Q_BLOCK), jnp.int32))

    cut = cut_ref[...]

    acc_ref[...] = jnp.zeros(acc_ref.shape, _F32)
    qt = qt_ref[0, 0]
    bound = bound_ref[0, 0]
    fixed_shift = 2.0 * bound <= MAX_FIXED_SHIFT_SPAN

    def selected(kt):
        start = pl.multiple_of(kt * tk, tk)
        keys = key_ref[pl.ds(start, tk), :]
        sel = jnp.logical_or(keys > thr, jnp.logical_and(keys == thr, start + row <= cut))
        return start, jnp.logical_and(sel, keys != INT_MIN)

    @pl.when(fixed_shift)
    def _():
        def att_tile(kt, carry):
            start, sel = selected(kt)
            mk = jnp.where(sel, -bound, NEG_BIG)
            lg = _dot(k_ref[0, pl.ds(start, tk), :], qt)
            for h in range(A_HEADS):
                sl = slice(h * Q_BLOCK, (h + 1) * Q_BLOCK)
                p_ref[:, sl] = jnp.exp2(lg[:, sl] + mk).astype(_BF)
            acc_ref[...] += _dot(vt_ref[0, :, pl.ds(start, tk)], p_ref[...])
            return carry

        lax.fori_loop(0, n_kt, att_tile, 0)

    @pl.when(jnp.logical_not(fixed_shift))
    def _():
        m_ref[...] = jnp.full(m_ref.shape, NEG_BIG, _F32)

        def att_tile(kt, carry):
            start, sel = selected(kt)
            mk = jnp.where(sel, 0.0, NEG_BIG)
            lg = _dot(k_ref[0, pl.ds(start, tk), :], qt)
            for h in range(A_HEADS):
                sl = slice(h * Q_BLOCK, (h + 1) * Q_BLOCK)
                x = lg[:, sl] + mk
                m_old = m_ref[:, sl]
                m_new = jnp.maximum(m_old, jnp.max(x, axis=0, keepdims=True))
                p_ref[:, sl] = jnp.exp2(x - m_new).astype(_BF)
                alpha_ref[:, sl] = jnp.exp2(m_old - m_new)
                m_ref[:, sl] = m_new
            pv = _dot(vt_ref[0, :, pl.ds(start, tk)], p_ref[...])
            acc_ref[...] = acc_ref[...] * alpha_ref[...] + pv
            return carry

        lax.fori_loop(0, n_kt, att_tile, 0)

    for h in range(A_HEADS):
        sl = slice(h * Q_BLOCK, (h + 1) * Q_BLOCK)
        out_h = acc_ref[0:A_DHEAD, sl] / acc_ref[A_DHEAD:A_DHEAD + 1, sl]
        o_ref[0, :, h * A_DHEAD:(h + 1) * A_DHEAD] = out_h.T.astype(o_ref.dtype)


def _dsa_attend(bound, ki, k, vt, qit, wi, qt, *, name):
    b, t, _ = ki.shape
    nb = t // Q_BLOCK
    hq = A_HEADS * Q_BLOCK
    topk = min(TOPK_MAX, t // 4)
    kern = functools.partial(_dsa_kernel, tk=KEY_TILE, topk=topk)
    return pl.pallas_call(
        kern,
        grid=(b, nb),
        in_specs=[
            pl.BlockSpec(memory_space=pltpu.SMEM),
            pl.BlockSpec((1, t, IDX_DIM), lambda i, j: (i, 0, 0)),
            pl.BlockSpec((1, t, A_DHEAD), lambda i, j: (i, 0, 0)),
            pl.BlockSpec((1, vt.shape[1], t), lambda i, j: (i, 0, 0)),
            pl.BlockSpec((1, 1, IDX_DIM, hq), lambda i, j: (i, j, 0, 0)),
            pl.BlockSpec((1, 1, 1, hq), lambda i, j: (i, j, 0, 0)),
            pl.BlockSpec((1, 1, A_DHEAD, hq), lambda i, j: (i, j, 0, 0)),
        ],
        out_specs=pl.BlockSpec((1, Q_BLOCK, A_HEADS * A_DHEAD), lambda i, j: (i, j, 0)),
        out_shape=jax.ShapeDtypeStruct((b, t, A_HEADS * A_DHEAD), _BF),
        scratch_shapes=[
            pltpu.VMEM((t, Q_BLOCK), jnp.int32),
            pltpu.VMEM((t, Q_BLOCK), jnp.int16),
            pltpu.VMEM((t, Q_BLOCK), jnp.int16),
            pltpu.VMEM((SUBLANES, Q_BLOCK), jnp.int32),
            pltpu.VMEM((1, Q_BLOCK), jnp.int32),
            pltpu.VMEM((vt.shape[1], hq), _F32),
            pltpu.VMEM((1, hq), _F32),
            pltpu.VMEM((1, hq), _F32),
            pltpu.VMEM((KEY_TILE, hq), _BF),
        ],
        compiler_params=_params("parallel", "arbitrary"),
        name=name,
    )(bound.reshape(1, 1), ki, k, vt, qit, wi, qt)


def _mlstm_layer(x2, norm_g, w_in_bf, w_gate, gate_b, head_g, w_out_bf, *, layer, batch, seq):
    n, d = x2.shape
    hk = M_HEADS * M_DQK
    hv = M_HEADS * M_DV
    main = 2 * hk + 2 * hv
    proj = _norm_matmul(x2, norm_g, w_in_bf, layer=layer, cols=main, tm=1024, tn=1024, name="mlstm_in")
    gates = _gates(x2, norm_g, w_gate, gate_b, tm=512, name="mlstm_gates")
    gates = gates.reshape(batch, seq, 2 * M_HEADS)
    h = _mlstm_core(proj.reshape(batch, seq, main), gates, gates.transpose(0, 2, 1), head_g, name="mlstm_core")
    return _matmul_resid(h.reshape(n, hv), w_out_bf, x2, layer=layer, tm=512, tn=d, name="mlstm_out")


def _dsa_layer(x2, norm_g, w_in, q_g, k_g, w_out, *, batch, seq):
    n, d = x2.shape
    qd = A_HEADS * A_DHEAD
    qid = IDX_HEADS * IDX_DIM
    o_k, o_v, o_qi = qd, qd + A_DHEAD, qd + 2 * A_DHEAD
    o_ki, o_wi, o_end = o_qi + qid, o_qi + qid + IDX_DIM, o_qi + qid + IDX_DIM + IDX_HEADS
    tn = 512
    padded = -(-o_end // tn) * tn
    w_pad = jnp.zeros((1, d, padded), _BF).at[0, :, :o_end].set(w_in.astype(_BF))
    proj = _norm_matmul(x2, norm_g, w_pad, layer=0, cols=padded, tm=1024, tn=tn, name="dsa_in")
    qt, qit, kn, ki, vt = _dsa_prep(proj, q_g, k_g, batch=batch, seq=seq, offs=(o_k, o_v, o_qi, o_ki), name="dsa_qknorm")
    nb = seq // Q_BLOCK
    wi = proj[:, o_wi:o_end].reshape(batch, nb, Q_BLOCK, IDX_HEADS).transpose(0, 1, 3, 2).reshape(batch, nb, 1, IDX_HEADS * Q_BLOCK)
    bound = 1.01 * LOG2_E * A_DHEAD ** 0.5 * jnp.max(jnp.abs(q_g)) * jnp.max(jnp.abs(k_g))
    out = _dsa_attend(bound.astype(_F32), ki.reshape(batch, seq, IDX_DIM), kn.reshape(batch, seq, A_DHEAD), vt, qit, wi, qt, name="dsa_attend")
    return _matmul_resid(out.reshape(n, qd), w_out.astype(_BF)[None], x2, layer=0, tm=512, tn=d, name="dsa_out")


def _conv_layer(x2, norm_g, w_in, dw_w, dw_b, ln_g, ln_b, w_out, *, seq):
    u = _norm_glu(x2, norm_g, w_in.astype(_BF), tm=1024, tn=512, name="conv_in")
    v = _conv_mid(u, dw_w, dw_b, ln_g, ln_b, seq=seq, tm=256, name="conv_mid")
    return _matmul_resid(v, w_out.astype(_BF)[None], x2, layer=0, tm=512, tn=x2.shape[1], name="conv_out")


def kernel(x, mix_norm_g, ffn_norm_g, mlstm_w_in, mlstm_gate_b, mlstm_head_g, mlstm_w_out,
           dsa_w_in, dsa_q_g, dsa_k_g, dsa_w_out, conv_w_in, conv_dw_w, conv_dw_b,
           conv_ln_g, conv_ln_b, conv_w_out, ffn_w_up, ffn_dw_w, ffn_w_down):
    batch, seq, d = x.shape
    depth = mix_norm_g.shape[0]
    x2 = x.reshape(batch * seq, d)
    main = 2 * M_HEADS * (M_DQK + M_DV)
    mlstm_w_in_bf = mlstm_w_in.astype(_BF)
    mlstm_w_out_bf = mlstm_w_out.astype(_BF)
    ffn_w_up_bf = ffn_w_up.astype(_BF)
    ffn_w_down_bf = ffn_w_down.astype(_BF)
    for i in range(depth):
        kind = i % N_MIXERS
        j = i // N_MIXERS
        if kind == 0:
            x2 = _mlstm_layer(x2, mix_norm_g[i], mlstm_w_in_bf, mlstm_w_in[j, :, main:], mlstm_gate_b[j], mlstm_head_g[j],
                              mlstm_w_out_bf, layer=j, batch=batch, seq=seq)
        elif kind == 1:
            x2 = _dsa_layer(x2, mix_norm_g[i], dsa_w_in[j], dsa_q_g[j], dsa_k_g[j], dsa_w_out[j], batch=batch, seq=seq)
        else:
            x2 = _conv_layer(x2, mix_norm_g[i], conv_w_in[j], conv_dw_w[j], conv_dw_b[j], conv_ln_g[j], conv_ln_b[j],
                             conv_w_out[j], seq=seq)
        x2 = _ffn(x2, ffn_norm_g[i], ffn_w_up_bf, ffn_dw_w[i], ffn_w_down_bf, layer=i, seq=seq, tm=512, tn=512, name="ffn")
    return x2.reshape(batch, seq, d)
```
